```python
import jax, jax.numpy as jnp
from jax import lax
import numpy as np

D_MODEL = 1024
BATCH = 16
SEQ = 2048
DEPTH = 2

N_META = 16
POOL_WINDOWS = (2, 4, 8, 16)
POOL_GROUP = 128
POOL_WIDTH = POOL_GROUP * len(POOL_WINDOWS)
N_HEADS = 16
QK_NOPE = 64
QK_ROPE = 32
V_DIM = 64
Q_RANK = 256
KV_RANK = 128
QK_DIM = QK_NOPE + QK_ROPE
ATT_WIDTH = N_HEADS * V_DIM
SM_SCALE = QK_DIM ** -0.5
ROPE_THETA = 10000.0
Q_BLOCK = 128
D_FF = -(-8 * D_MODEL // (3 * 256)) * 256
NORM_EPS = 1e-6
MASK_VALUE = -1e30
IN_SIZES = (POOL_WIDTH, Q_RANK, KV_RANK, QK_ROPE, D_MODEL, D_MODEL)
D_IN = POOL_WIDTH + Q_RANK + KV_RANK + QK_ROPE + 2 * D_MODEL
IN_OFFSETS = (POOL_WIDTH,
              POOL_WIDTH + Q_RANK,
              POOL_WIDTH + Q_RANK + KV_RANK,
              POOL_WIDTH + Q_RANK + KV_RANK + QK_ROPE,
              POOL_WIDTH + Q_RANK + KV_RANK + QK_ROPE + D_MODEL)

kernel_name = "hybrid_pool_mla_gated_block"


def rmsnorm(x, g):
    xf = x.astype(jnp.float32)
    y = xf * lax.rsqrt(jnp.mean(xf * xf, axis=-1, keepdims=True) + NORM_EPS)
    return (y * g.astype(jnp.float32)).astype(x.dtype)


def rope_tables(length):
    inv = 1.0 / (ROPE_THETA ** (jnp.arange(0, QK_ROPE, 2, dtype=jnp.float32) / QK_ROPE))
    ang = jnp.arange(length, dtype=jnp.float32)[:, None] * inv[None, :]
    return jnp.cos(ang), jnp.sin(ang)


def apply_rope(x, cos, sin):
    xf = x.astype(jnp.float32)
    x1, x2 = jnp.split(xf, 2, axis=-1)
    out = jnp.concatenate([x1 * cos - x2 * sin, x1 * sin + x2 * cos], axis=-1)
    return out.astype(x.dtype)


def pool_mixer(u, pool_w, pool_scale):
    B, L, _ = u.shape
    cs = jnp.cumsum(u.astype(jnp.float32), axis=1)
    cs0 = jnp.concatenate([jnp.zeros((B, 1, POOL_WIDTH), jnp.float32), cs], axis=1)
    t = jnp.arange(L, dtype=jnp.float32)[:, None]
    groups = []
    for g, w in enumerate(POOL_WINDOWS):
        c = cs0[:, :, g * POOL_GROUP:(g + 1) * POOL_GROUP]
        prev = jnp.pad(c[:, :L + 1 - w], ((0, 0), (w, 0), (0, 0)))
        wsum = (c - prev)[:, 1:]
        count = jnp.minimum(t + 1.0, float(w))
        ug = u[:, :, g * POOL_GROUP:(g + 1) * POOL_GROUP].astype(jnp.float32)
        groups.append(wsum / count - ug)
    y = jnp.stack(groups, axis=2).astype(u.dtype)
    y = jnp.einsum('blgc,gcd->blgd', y, pool_w).reshape(B, L, POOL_WIDTH)
    return y * pool_scale


def mla_attention(q_nope, q_rope, k_nope, k_rope, v):
    L = q_nope.shape[1]
    outs = []
    for start in range(0, L, Q_BLOCK):
        end = min(start + Q_BLOCK, L)
        s = (jnp.einsum('bqhd,bkhd->bhqk', q_nope[:, start:end], k_nope[:, :end])
             + jnp.einsum('bqhr,bkr->bhqk', q_rope[:, start:end], k_rope[:, :end]))
        s = s.astype(jnp.float32) * SM_SCALE
        mask = jnp.arange(end)[None, :] <= jnp.arange(start, end)[:, None]
        s = jnp.where(mask[None, None], s, MASK_VALUE)
        p = jax.nn.softmax(s, axis=-1).astype(v.dtype)
        outs.append(jnp.einsum('bhqk,bkhd->bqhd', p, v[:, :end]))
    return jnp.concatenate(outs, axis=1)


def hybrid_layer(h, cos, sin, g_mix, w_in, pool_w, pool_scale, q_norm_g, kv_norm_g, w_uq, w_ukv,
                 w_pa, w_pb, w_o, g_ffn, w_gate, w_up, w_down):
    B, L, _ = h.shape
    hn = rmsnorm(h, g_mix)
    z = hn @ w_in
    u, c_q, c_kv, k_rope, gate_a, gate_b = jnp.split(z, IN_OFFSETS, axis=-1)
    a = pool_mixer(u, pool_w, pool_scale)
    q = (rmsnorm(c_q, q_norm_g) @ w_uq).reshape(B, L, N_HEADS, QK_DIM)
    q_nope, q_rope = q[..., :QK_NOPE], q[..., QK_NOPE:]
    q_rope = apply_rope(q_rope, cos[:, None, :], sin[:, None, :])
    kv = (rmsnorm(c_kv, kv_norm_g) @ w_ukv).reshape(B, L, N_HEADS, QK_NOPE + V_DIM)
    k_nope, v = kv[..., :QK_NOPE], kv[..., QK_NOPE:]
    k_rope = apply_rope(k_rope, cos, sin)
    b = mla_attention(q_nope, q_rope, k_nope, k_rope, v).reshape(B, L, ATT_WIDTH)
    merged = jax.nn.sigmoid(gate_a) * (a @ w_pa) + jax.nn.sigmoid(gate_b) * (b @ w_pb)
    h = h + merged @ w_o
    hn = rmsnorm(h, g_ffn)
    h = h + (jax.nn.silu(hn @ w_gate) * (hn @ w_up)) @ w_down
    return h


def setup_inputs(seed: int = 0) -> dict:
    key = jax.random.key(seed)
    ks = jax.random.split(key, 20)

    def w(k, shape, fan_in):
        return jax.random.normal(k, shape, jnp.float32) * (fan_in ** -0.5)

    def gain(k, shape):
        return 1.0 + 0.05 * jax.random.normal(k, shape, jnp.float32)

    return {
        "x": jax.random.normal(ks[0], (BATCH, SEQ, D_MODEL), jnp.float32),
        "meta_tokens": jax.random.normal(ks[1], (N_META, D_MODEL), jnp.float32),
        "norm_mix_g": gain(ks[2], (DEPTH, D_MODEL)),
        "w_in": w(ks[3], (DEPTH, D_MODEL, D_IN), D_MODEL),
        "pool_w": w(ks[4], (DEPTH, len(POOL_WINDOWS), POOL_GROUP, POOL_GROUP), POOL_GROUP),
        "pool_scale": gain(ks[5], (DEPTH, POOL_WIDTH)),
        "q_norm_g": gain(ks[6], (DEPTH, Q_RANK)),
        "kv_norm_g": gain(ks[7], (DEPTH, KV_RANK)),
        "w_uq": w(ks[8], (DEPTH, Q_RANK, N_HEADS * QK_DIM), Q_RANK),
        "w_ukv": w(ks[9], (DEPTH, KV_RANK, N_HEADS * (QK_NOPE + V_DIM)), KV_RANK),
        "w_pa": w(ks[10], (DEPTH, POOL_WIDTH, D_MODEL), POOL_WIDTH),
        "w_pb": w(ks[11], (DEPTH, ATT_WIDTH, D_MODEL), ATT_WIDTH),
        "w_o": w(ks[12], (DEPTH, D_MODEL, D_MODEL), D_MODEL),
        "norm_ffn_g": gain(ks[13], (DEPTH, D_MODEL)),
        "w_gate": w(ks[14], (DEPTH, D_MODEL, D_FF), D_MODEL),
        "w_up": w(ks[15], (DEPTH, D_MODEL, D_FF), D_MODEL),
        "w_down": w(ks[16], (DEPTH, D_FF, D_MODEL), D_FF),
        "final_norm_g": gain(ks[17], (D_MODEL,)),
    }


def reference(x, meta_tokens, norm_mix_g, w_in, pool_w, pool_scale, q_norm_g, kv_norm_g, w_uq, w_ukv,
              w_pa, w_pb, w_o, norm_ffn_g, w_gate, w_up, w_down, final_norm_g):
    B = x.shape[0]
    meta = jnp.broadcast_to(meta_tokens.astype(x.dtype)[None], (B, N_META, D_MODEL))
    h = jnp.concatenate([meta, x], axis=1)
    cos, sin = rope_tables(h.shape[1])
    for i in range(DEPTH):
        h = hybrid_layer(h, cos, sin, norm_mix_g[i], w_in[i], pool_w[i], pool_scale[i], q_norm_g[i],
                         kv_norm_g[i], w_uq[i], w_ukv[i], w_pa[i], w_pb[i], w_o[i], norm_ffn_g[i],
                         w_gate[i], w_up[i], w_down[i])
    return rmsnorm(h, final_norm_g)[:, N_META:]
```

```python
import functools
import math

import jax
import jax.numpy as jnp
from jax import lax
from jax.experimental import pallas as pl
from jax.experimental.pallas import tpu as pltpu

D_MODEL = 1024
N_META = 16
POOL_WINDOWS = (2, 4, 8, 16)
POOL_GROUP = 128
POOL_WIDTH = POOL_GROUP * len(POOL_WINDOWS)
N_HEADS = 16
QK_NOPE = 64
QK_ROPE = 32
V_DIM = 64
Q_RANK = 256
KV_RANK = 128
QK_DIM = QK_NOPE + QK_ROPE
ATT_WIDTH = N_HEADS * V_DIM
ROPE_THETA = 10000.0
D_FF = 2816
NORM_EPS = 1e-6
MASK_VALUE = -1e30

LANES = 128
HEAD_PAD = LANES
MAX_WINDOW = max(POOL_WINDOWS)
Q_SCALE = (QK_DIM ** -0.5) * math.log2(math.e)
META_TILE = 128
VMEM_LIMIT = 56 * 1024 * 1024

_NT = (((1,), (1,)), ((), ()))


def _rmsnorm(x, g):
    return x * lax.rsqrt(jnp.mean(x * x, axis=-1, keepdims=True) + NORM_EPS) * g


def _dot(a, b):
    return jnp.dot(a, b, preferred_element_type=jnp.float32)


def _const_spec(shape):
    nd = len(shape)
    return pl.BlockSpec(shape, lambda *_: (0,) * nd, pipeline_mode=pl.Buffered(1))


def _pre_attention_body(h_ref, gmix_ref, win_ref, poolw_ref, pscale_ref, qg_ref, kvg_ref,
                        wq_ref, wk_ref, wvt_ref, cos_ref, sin_lo_ref, sin_hi_ref, halo_ref,
                        q_out, k_out, vt_out, a_out, *rest, tile, pos_offset, emit_u):
    if emit_u:
        u_out, ue_scr = rest
    else:
        (ue_scr,) = rest
    j = pl.program_id(1)
    row0 = pl.multiple_of(j * tile, tile)

    hn = _rmsnorm(h_ref[...], gmix_ref[...]).astype(jnp.bfloat16)
    z = _dot(hn, win_ref[...])
    u = z[:, :POOL_WIDTH]
    c_q = z[:, POOL_WIDTH:POOL_WIDTH + Q_RANK]
    c_kv = z[:, POOL_WIDTH + Q_RANK:POOL_WIDTH + Q_RANK + KV_RANK]
    kr = z[:, POOL_WIDTH + Q_RANK + KV_RANK:]

    cos = cos_ref[pl.ds(row0, tile), :]
    sin_lo = sin_lo_ref[pl.ds(row0, tile), :]
    sin_hi = sin_hi_ref[pl.ds(row0, tile), :]

    def rope(x):
        return (x * cos + pltpu.roll(x, LANES - QK_ROPE // 2, 1) * sin_lo
                + pltpu.roll(x, QK_ROPE // 2, 1) * sin_hi)

    cqn = _rmsnorm(c_q, qg_ref[...]).astype(jnp.bfloat16)
    q_raw = _dot(cqn, wq_ref[...])
    for hd in range(N_HEADS):
        x = q_raw[:, hd * HEAD_PAD:(hd + 1) * HEAD_PAD]
        q_out[hd] = (rope(x) * Q_SCALE).astype(jnp.bfloat16)

    ckvn = _rmsnorm(c_kv, kvg_ref[...]).astype(jnp.bfloat16)
    k_raw = _dot(ckvn, wk_ref[...])
    kr_roped = rope(kr)
    for hd in range(N_HEADS):
        k_out[hd] = (k_raw[:, hd * HEAD_PAD:(hd + 1) * HEAD_PAD] + kr_roped).astype(jnp.bfloat16)
    vt = lax.dot_general(wvt_ref[...], ckvn, _NT, preferred_element_type=jnp.float32)
    vt_out[0] = vt.astype(jnp.bfloat16)

    @pl.when(j == 0)
    def _():
        ue_scr[0:MAX_WINDOW, :] = halo_ref[...]
    ue_scr[MAX_WINDOW:MAX_WINDOW + tile, :] = u
    if emit_u:
        u_out[...] = u
    pos = pos_offset + row0 + lax.broadcasted_iota(jnp.int32, (tile, POOL_GROUP), 0)
    outs = []
    for g, w in enumerate(POOL_WINDOWS):
        lanes = slice(g * POOL_GROUP, (g + 1) * POOL_GROUP)
        wsum = ue_scr[MAX_WINDOW:MAX_WINDOW + tile, lanes]
        for d in range(1, w):
            wsum = wsum + ue_scr[MAX_WINDOW - d:MAX_WINDOW - d + tile, lanes]
        count = jnp.minimum(pos + 1, w).astype(jnp.float32)
        y = (wsum / count - u[:, lanes]).astype(jnp.bfloat16)
        outs.append(_dot(y, poolw_ref[g]))
    a = jnp.concatenate(outs, axis=1) * pscale_ref[...]
    a_out[...] = a.astype(jnp.bfloat16)
    ue_scr[0:MAX_WINDOW, :] = ue_scr[tile:tile + MAX_WINDOW, :]


def _pre_attention(h, lw, tables, halo, *, batch, seq, tile, pos_offset, emit_u):
    rows = batch * seq
    n_tiles = seq // tile
    row_map = lambda b, j: (b * n_tiles + j, 0)
    in_specs = [
        pl.BlockSpec((tile, D_MODEL), row_map),
        _const_spec((1, D_MODEL)),
        _const_spec((D_MODEL, 1024)),
        _const_spec((len(POOL_WINDOWS), POOL_GROUP, POOL_GROUP)),
        _const_spec((1, POOL_WIDTH)),
        _const_spec((1, Q_RANK)),
        _const_spec((1, KV_RANK)),
        _const_spec((Q_RANK, N_HEADS * HEAD_PAD)),
        _const_spec((KV_RANK, N_HEADS * HEAD_PAD)),
        _const_spec((ATT_WIDTH, KV_RANK)),
        _const_spec((seq, LANES)),
        _const_spec((seq, LANES)),
        _const_spec((seq, LANES)),
        pl.BlockSpec((MAX_WINDOW, POOL_WIDTH), lambda b, j: (0, 0)),
    ]
    out_shape = [
        jax.ShapeDtypeStruct((N_HEADS, rows, HEAD_PAD), jnp.bfloat16),
        jax.ShapeDtypeStruct((N_HEADS, rows, HEAD_PAD), jnp.bfloat16),
        jax.ShapeDtypeStruct((rows // tile, ATT_WIDTH, tile), jnp.bfloat16),
        jax.ShapeDtypeStruct((rows, POOL_WIDTH), jnp.bfloat16),
    ]
    out_specs = [
        pl.BlockSpec((N_HEADS, tile, HEAD_PAD), lambda b, j: (0, b * n_tiles + j, 0)),
        pl.BlockSpec((N_HEADS, tile, HEAD_PAD), lambda b, j: (0, b * n_tiles + j, 0)),
        pl.BlockSpec((1, ATT_WIDTH, tile), lambda b, j: (b * n_tiles + j, 0, 0)),
        pl.BlockSpec((tile, POOL_WIDTH), row_map),
    ]
    if emit_u:
        out_shape.append(jax.ShapeDtypeStruct((rows, POOL_WIDTH), jnp.float32))
        out_specs.append(pl.BlockSpec((tile, POOL_WIDTH), row_map))
    body = functools.partial(_pre_attention_body, tile=tile, pos_offset=pos_offset, emit_u=emit_u)
    return pl.pallas_call(
        body,
        grid=(batch, n_tiles),
        in_specs=in_specs,
        out_specs=out_specs,
        out_shape=out_shape,
        scratch_shapes=[pltpu.VMEM((tile + MAX_WINDOW, POOL_WIDTH), jnp.float32)],
        compiler_params=pltpu.CompilerParams(
            dimension_semantics=("arbitrary", "arbitrary"), vmem_limit_bytes=VMEM_LIMIT),
        name="pre_attention",
    )(h, lw["g_mix"], lw["w_in_a"], lw["pool_w"], lw["pool_scale"], lw["q_g"], lw["kv_g"],
      lw["w_q"], lw["w_k"], lw["w_vt"], tables[0], tables[1], tables[2], halo)


def _attention_body(*refs, tq, tk, has_meta):
    if has_meta:
        q_ref, k_ref, vt_ref, km_ref, vtm_ref, o_ref, m_scr, l_scr, acc_scr, ot_scr = refs
    else:
        q_ref, k_ref, vt_ref, o_ref, m_scr, l_scr, acc_scr, ot_scr = refs
    i = pl.program_id(1)

    def block(q, k, vt, mask):
        s = lax.dot_general(k, q, _NT, preferred_element_type=jnp.float32)
        if mask is not None:
            s = jnp.where(mask, s, MASK_VALUE)
        m_old = m_scr[...]
        m_new = jnp.maximum(m_old, jnp.max(s, axis=0, keepdims=True))
        alpha = jnp.exp2(m_old - m_new)
        p = jnp.exp2(s - m_new)
        l_scr[...] = alpha * l_scr[...] + jnp.sum(p, axis=0, keepdims=True)
        acc_scr[...] = alpha * acc_scr[...] + _dot(vt, p.astype(jnp.bfloat16))
        m_scr[...] = m_new

    key_idx = lax.broadcasted_iota(jnp.int32, (tk, tq), 0)
    qry_idx = lax.broadcasted_iota(jnp.int32, (tk, tq), 1)

    def head(hd, carry):
        q = q_ref[hd]
        m_scr[...] = jnp.full_like(m_scr, MASK_VALUE)
        l_scr[...] = jnp.zeros_like(l_scr)
        acc_scr[...] = jnp.zeros_like(acc_scr)
        v_rows = pl.ds(pl.multiple_of(hd * V_DIM, V_DIM), V_DIM)
        if has_meta:
            block(q, km_ref[hd, 0:N_META, :], vtm_ref[0, v_rows, 0:N_META], None)

        def full(jb, c):
            k = k_ref[hd, pl.ds(pl.multiple_of(jb * tk, tk), tk), :]
            block(q, k, vt_ref[jb, v_rows, :], None)
            return c
        n_full = i * (tq // tk)
        lax.fori_loop(0, n_full, full, 0)

        for sb in range(tq // tk):
            jb = n_full + sb
            k = k_ref[hd, pl.ds(pl.multiple_of(jb * tk, tk), tk), :]
            block(q, k, vt_ref[jb, v_rows, :], key_idx + sb * tk <= qry_idx)
        ot_scr[hd] = acc_scr[...] / l_scr[...]
        return carry

    lax.fori_loop(0, N_HEADS, head, 0)

    for pair in range(N_HEADS // 2):
        both = jnp.concatenate([ot_scr[2 * pair], ot_scr[2 * pair + 1]], axis=0)
        o_ref[:, pair * LANES:(pair + 1) * LANES] = both.T.astype(jnp.bfloat16)


def _attention(q, k, vt, meta_kv, *, batch, seq, tq, tk):
    rows = batch * seq
    nq = seq // tq
    nk = seq // tk
    has_meta = meta_kv is not None
    in_specs = [
        pl.BlockSpec((N_HEADS, tq, HEAD_PAD), lambda b, i: (0, b * nq + i, 0)),
        pl.BlockSpec((N_HEADS, seq, HEAD_PAD), lambda b, i: (0, b, 0)),
        pl.BlockSpec((nk, ATT_WIDTH, tk), lambda b, i: (b, 0, 0)),
    ]
    args = [q, k, vt]
    if has_meta:
        km, vtm = meta_kv
        in_specs += [_const_spec(km.shape), _const_spec(vtm.shape)]
        args += [km, vtm]
    body = functools.partial(_attention_body, tq=tq, tk=tk, has_meta=has_meta)
    return pl.pallas_call(
        body,
        grid=(batch, nq),
        in_specs=in_specs,
        out_specs=pl.BlockSpec((tq, ATT_WIDTH), lambda b, i: (b * nq + i, 0)),
        out_shape=jax.ShapeDtypeStruct((rows, ATT_WIDTH), jnp.bfloat16),
        scratch_shapes=[
            pltpu.VMEM((1, tq), jnp.float32),
            pltpu.VMEM((1, tq), jnp.float32),
            pltpu.VMEM((V_DIM, tq), jnp.float32),
            pltpu.VMEM((N_HEADS, V_DIM, tq), jnp.float32),
        ],
        compiler_params=pltpu.CompilerParams(
            dimension_semantics=("parallel", "parallel"), vmem_limit_bytes=VMEM_LIMIT),
        name="attention",
    )(*args)


def _post_attention_body(h_ref, a_ref, b_ref, gmix_ref, wgates_ref, wpa_ref, wpb_ref, wo_ref,
                         gffn_ref, wgate_ref, wup_ref, wdown_ref, gfin_ref, o_ref, *, final_norm):
    h = h_ref[...]
    hn = _rmsnorm(h, gmix_ref[...]).astype(jnp.bfloat16)
    gates = jax.nn.sigmoid(_dot(hn, wgates_ref[...]))
    merged = (gates[:, :D_MODEL] * _dot(a_ref[...], wpa_ref[...])
              + gates[:, D_MODEL:] * _dot(b_ref[...], wpb_ref[...]))
    h = h + _dot(merged.astype(jnp.bfloat16), wo_ref[...])
    hn = _rmsnorm(h, gffn_ref[...]).astype(jnp.bfloat16)
    act = jax.nn.silu(_dot(hn, wgate_ref[...])) * _dot(hn, wup_ref[...])
    h = h + _dot(act.astype(jnp.bfloat16), wdown_ref[...])
    if final_norm:
        h = _rmsnorm(h, gfin_ref[...])
    o_ref[...] = h


def _post_attention(h, a, b, lw, g_final, *, tile, final_norm):
    rows = h.shape[0]
    row_map = lambda r: (r, 0)
    body = functools.partial(_post_attention_body, final_norm=final_norm)
    return pl.pallas_call(
        body,
        grid=(rows // tile,),
        in_specs=[
            pl.BlockSpec((tile, D_MODEL), row_map),
            pl.BlockSpec((tile, POOL_WIDTH), row_map),
            pl.BlockSpec((tile, ATT_WIDTH), row_map),
            _const_spec((1, D_MODEL)),
            _const_spec((D_MODEL, 2 * D_MODEL)),
            _const_spec((POOL_WIDTH, D_MODEL)),
            _const_spec((ATT_WIDTH, D_MODEL)),
            _const_spec((D_MODEL, D_MODEL)),
            _const_spec((1, D_MODEL)),
            _const_spec((D_MODEL, D_FF)),
            _const_spec((D_MODEL, D_FF)),
            _const_spec((D_FF, D_MODEL)),
            _const_spec((1, D_MODEL)),
        ],
        out_specs=pl.BlockSpec((tile, D_MODEL), row_map),
        out_shape=jax.ShapeDtypeStruct((rows, D_MODEL), jnp.float32),
        compiler_params=pltpu.CompilerParams(
            dimension_semantics=("parallel",), vmem_limit_bytes=VMEM_LIMIT),
        name="post_attention",
    )(h, a, b, lw["g_mix"], lw["w_in_gates"], lw["w_pa"], lw["w_pb"], lw["w_o"], lw["g_ffn"],
      lw["w_gate"], lw["w_up"], lw["w_down"], g_final)


def _layer_weights(i, norm_mix_g, w_in, pool_w, pool_scale, q_norm_g, kv_norm_g, w_uq, w_ukv,
                   w_pa, w_pb, w_o, norm_ffn_g, w_gate, w_up, w_down):
    bf = jnp.bfloat16
    o_q = POOL_WIDTH + Q_RANK
    o_kv = o_q + KV_RANK
    o_kr = o_kv + QK_ROPE
    wi = w_in[i]
    kr_cols = jnp.pad(wi[:, o_kv:o_kr], ((0, 0), (QK_NOPE, LANES - QK_DIM)))
    w_in_a = jnp.concatenate([wi[:, :o_kv], kr_cols], axis=1).astype(bf)
    w_q = jnp.pad(w_uq[i].reshape(Q_RANK, N_HEADS, QK_DIM),
                  ((0, 0), (0, 0), (0, HEAD_PAD - QK_DIM))).reshape(Q_RANK, N_HEADS * HEAD_PAD)
    ukv = w_ukv[i].reshape(KV_RANK, N_HEADS, QK_NOPE + V_DIM)
    w_k = jnp.pad(ukv[:, :, :QK_NOPE],
                  ((0, 0), (0, 0), (0, HEAD_PAD - QK_NOPE))).reshape(KV_RANK, N_HEADS * HEAD_PAD)
    w_vt = ukv[:, :, QK_NOPE:].reshape(KV_RANK, ATT_WIDTH).T
    return dict(
        g_mix=norm_mix_g[i][None], w_in_a=w_in_a, w_in_gates=wi[:, o_kr:].astype(bf),
        pool_w=pool_w[i].astype(bf), pool_scale=pool_scale[i][None],
        q_g=q_norm_g[i][None], kv_g=kv_norm_g[i][None],
        w_q=w_q.astype(bf), w_k=w_k.astype(bf), w_vt=w_vt.astype(bf),
        w_pa=w_pa[i].astype(bf), w_pb=w_pb[i].astype(bf), w_o=w_o[i].astype(bf),
        g_ffn=norm_ffn_g[i][None], w_gate=w_gate[i].astype(bf), w_up=w_up[i].astype(bf),
        w_down=w_down[i].astype(bf))


def _rope_tables(length):
    inv = 1.0 / (ROPE_THETA ** (jnp.arange(0, QK_ROPE, 2, dtype=jnp.float32) / QK_ROPE))
    ang = jnp.arange(length, dtype=jnp.float32)[:, None] * inv[None, :]
    cos, sin = jnp.cos(ang), jnp.sin(ang)
    half = QK_ROPE // 2
    zeros = jnp.zeros((length, half), jnp.float32)
    pad = jnp.zeros((length, LANES - QK_DIM), jnp.float32)
    cos_t = jnp.concatenate([jnp.ones((length, QK_NOPE), jnp.float32), cos, cos, pad], axis=1)
    sin_lo = jnp.concatenate([jnp.zeros((length, QK_NOPE), jnp.float32), -sin, zeros, pad], axis=1)
    sin_hi = jnp.concatenate([jnp.zeros((length, QK_NOPE), jnp.float32), zeros, sin, pad], axis=1)
    return cos_t, sin_lo, sin_hi


def kernel(x, meta_tokens, norm_mix_g, w_in, pool_w, pool_scale, q_norm_g, kv_norm_g, w_uq, w_ukv,
           w_pa, w_pb, w_o, norm_ffn_g, w_gate, w_up, w_down, final_norm_g):
    batch, seq, _ = x.shape
    depth = w_in.shape[0]
    tile = 512
    tables = _rope_tables(N_META + max(seq, META_TILE))
    meta_tables = tuple(t[:META_TILE] for t in tables)
    main_tables = tuple(t[N_META:N_META + seq] for t in tables)
    g_final = final_norm_g[None]

    h = x.reshape(batch * seq, D_MODEL)
    hm = jnp.pad(meta_tokens.astype(x.dtype), ((0, META_TILE - N_META), (0, 0)))
    zero_halo = jnp.zeros((MAX_WINDOW, POOL_WIDTH), jnp.float32)
    for i in range(depth):
        lw = _layer_weights(i, norm_mix_g, w_in, pool_w, pool_scale, q_norm_g, kv_norm_g, w_uq,
                            w_ukv, w_pa, w_pb, w_o, norm_ffn_g, w_gate, w_up, w_down)
        last = i == depth - 1
        qm, km, vtm, am, um = _pre_attention(hm, lw, meta_tables, zero_halo, batch=1,
                                             seq=META_TILE, tile=META_TILE, pos_offset=0,
                                             emit_u=True)
        if not last:
            bm = _attention(qm, km, vtm, None, batch=1, seq=META_TILE, tq=META_TILE, tk=META_TILE)
            hm = _post_attention(hm, am, bm, lw, g_final, tile=META_TILE, final_norm=False)
        q, k, vt, a = _pre_attention(h, lw, main_tables, um, batch=batch, seq=seq, tile=tile,
                                     pos_offset=N_META, emit_u=False)
        b = _attention(q, k, vt, (km, vtm), batch=batch, seq=seq, tq=tile, tk=tile)
        h = _post_attention(h, a, b, lw, g_final, tile=tile, final_norm=last)
    return h.reshape(batch, seq, D_MODEL)
```

```python
import functools
import math

import jax
import jax.numpy as jnp
from jax import lax
from jax.experimental import pallas as pl
from jax.experimental.pallas import tpu as pltpu

D_MODEL = 1024
N_META = 16
POOL_WINDOWS = (2, 4, 8, 16)
POOL_GROUP = 128
POOL_WIDTH = POOL_GROUP * len(POOL_WINDOWS)
N_HEADS = 16
QK_NOPE = 64
QK_ROPE = 32
V_DIM = 64
Q_RANK = 256
KV_RANK = 128
QK_DIM = QK_NOPE + QK_ROPE
ATT_WIDTH = N_HEADS * V_DIM
ROPE_THETA = 10000.0
D_FF = 2816
NORM_EPS = 1e-6
MASK_VALUE = -1e30

LANES = 128
HEAD_PAD = LANES
MAX_WINDOW = max(POOL_WINDOWS)
Q_SCALE = (QK_DIM ** -0.5) * math.log2(math.e)
META_TILE = 128
HEAD_GROUP = 4
ONES_ROWS = 16
VMEM_LIMIT = 56 * 1024 * 1024

_NT = (((1,), (1,)), ((), ()))


def _rmsnorm(x, g):
    return x * lax.rsqrt(jnp.mean(x * x, axis=-1, keepdims=True) + NORM_EPS) * g


def _dot(a, b):
    return jnp.dot(a, b, preferred_element_type=jnp.float32)


def _const_spec(shape):
    nd = len(shape)
    return pl.BlockSpec(shape, lambda *_: (0,) * nd, pipeline_mode=pl.Buffered(1))


def _pre_attention_body(h_ref, gmix_ref, win_ref, poolw_ref, pscale_ref, qg_ref, kvg_ref,
                        wq_ref, wk_ref, wvt_ref, cos_ref, sin_lo_ref, sin_hi_ref, halo_ref,
                        q_out, k_out, vt_out, a_out, *rest, tile, pos_offset, emit_u):
    if emit_u:
        u_out, ue_scr = rest
    else:
        (ue_scr,) = rest
    j = pl.program_id(1)
    row0 = pl.multiple_of(j * tile, tile)

    hn = _rmsnorm(h_ref[...], gmix_ref[...]).astype(jnp.bfloat16)
    z = _dot(hn, win_ref[...])
    u = z[:, :POOL_WIDTH]
    c_q = z[:, POOL_WIDTH:POOL_WIDTH + Q_RANK]
    c_kv = z[:, POOL_WIDTH + Q_RANK:POOL_WIDTH + Q_RANK + KV_RANK]
    kr = z[:, POOL_WIDTH + Q_RANK + KV_RANK:]

    cos = cos_ref[pl.ds(row0, tile), :]
    sin_lo = sin_lo_ref[pl.ds(row0, tile), :]
    sin_hi = sin_hi_ref[pl.ds(row0, tile), :]

    def rope(x):
        return (x * cos + pltpu.roll(x, LANES - QK_ROPE // 2, 1) * sin_lo
                + pltpu.roll(x, QK_ROPE // 2, 1) * sin_hi)

    cqn = _rmsnorm(c_q, qg_ref[...]).astype(jnp.bfloat16)
    q_raw = _dot(cqn, wq_ref[...])
    for hd in range(N_HEADS):
        x = q_raw[:, hd * HEAD_PAD:(hd + 1) * HEAD_PAD]
        q_out[hd] = (rope(x) * Q_SCALE).astype(jnp.bfloat16)

    ckvn = _rmsnorm(c_kv, kvg_ref[...]).astype(jnp.bfloat16)
    k_raw = _dot(ckvn, wk_ref[...])
    kr_roped = rope(kr)
    for hd in range(N_HEADS):
        k_out[hd] = (k_raw[:, hd * HEAD_PAD:(hd + 1) * HEAD_PAD] + kr_roped).astype(jnp.bfloat16)
    vt = lax.dot_general(wvt_ref[...], ckvn, _NT, preferred_element_type=jnp.float32)
    vt_out[0] = vt.astype(jnp.bfloat16)

    @pl.when(j == 0)
    def _():
        ue_scr[0:MAX_WINDOW, :] = halo_ref[...]
    ue_scr[MAX_WINDOW:MAX_WINDOW + tile, :] = u
    if emit_u:
        u_out[...] = u
    pos = pos_offset + row0 + lax.broadcasted_iota(jnp.int32, (tile, POOL_GROUP), 0)
    outs = []
    for g, w in enumerate(POOL_WINDOWS):
        lanes = slice(g * POOL_GROUP, (g + 1) * POOL_GROUP)
        wsum = ue_scr[MAX_WINDOW:MAX_WINDOW + tile, lanes]
        for d in range(1, w):
            wsum = wsum + ue_scr[MAX_WINDOW - d:MAX_WINDOW - d + tile, lanes]
        count = jnp.minimum(pos + 1, w).astype(jnp.float32)
        y = (wsum / count - u[:, lanes]).astype(jnp.bfloat16)
        outs.append(_dot(y, poolw_ref[g]))
    a = jnp.concatenate(outs, axis=1) * pscale_ref[...]
    a_out[...] = a.astype(jnp.bfloat16)
    ue_scr[0:MAX_WINDOW, :] = ue_scr[tile:tile + MAX_WINDOW, :]


def _pre_attention(h, lw, tables, halo, *, batch, seq, tile, pos_offset, emit_u):
    rows = batch * seq
    n_tiles = seq // tile
    row_map = lambda b, j: (b * n_tiles + j, 0)
    in_specs = [
        pl.BlockSpec((tile, D_MODEL), row_map),
        _const_spec((1, D_MODEL)),
        _const_spec((D_MODEL, 1024)),
        _const_spec((len(POOL_WINDOWS), POOL_GROUP, POOL_GROUP)),
        _const_spec((1, POOL_WIDTH)),
        _const_spec((1, Q_RANK)),
        _const_spec((1, KV_RANK)),
        _const_spec((Q_RANK, N_HEADS * HEAD_PAD)),
        _const_spec((KV_RANK, N_HEADS * HEAD_PAD)),
        _const_spec((ATT_WIDTH, KV_RANK)),
        _const_spec((seq, LANES)),
        _const_spec((seq, LANES)),
        _const_spec((seq, LANES)),
        pl.BlockSpec((MAX_WINDOW, POOL_WIDTH), lambda b, j: (0, 0)),
    ]
    out_shape = [
        jax.ShapeDtypeStruct((N_HEADS, rows, HEAD_PAD), jnp.bfloat16),
        jax.ShapeDtypeStruct((N_HEADS, rows, HEAD_PAD), jnp.bfloat16),
        jax.ShapeDtypeStruct((rows // tile, ATT_WIDTH, tile), jnp.bfloat16),
        jax.ShapeDtypeStruct((rows, POOL_WIDTH), jnp.bfloat16),
    ]
    out_specs = [
        pl.BlockSpec((N_HEADS, tile, HEAD_PAD), lambda b, j: (0, b * n_tiles + j, 0)),
        pl.BlockSpec((N_HEADS, tile, HEAD_PAD), lambda b, j: (0, b * n_tiles + j, 0)),
        pl.BlockSpec((1, ATT_WIDTH, tile), lambda b, j: (b * n_tiles + j, 0, 0)),
        pl.BlockSpec((tile, POOL_WIDTH), row_map),
    ]
    if emit_u:
        out_shape.append(jax.ShapeDtypeStruct((rows, POOL_WIDTH), jnp.float32))
        out_specs.append(pl.BlockSpec((tile, POOL_WIDTH), row_map))
    body = functools.partial(_pre_attention_body, tile=tile, pos_offset=pos_offset, emit_u=emit_u)
    return pl.pallas_call(
        body,
        grid=(batch, n_tiles),
        in_specs=in_specs,
        out_specs=out_specs,
        out_shape=out_shape,
        scratch_shapes=[pltpu.VMEM((tile + MAX_WINDOW, POOL_WIDTH), jnp.float32)],
        compiler_params=pltpu.CompilerParams(
            dimension_semantics=("arbitrary", "arbitrary"), vmem_limit_bytes=VMEM_LIMIT),
        name="pre_attention",
    )(h, lw["g_mix"], lw["w_in_a"], lw["pool_w"], lw["pool_scale"], lw["q_g"], lw["kv_g"],
      lw["w_q"], lw["w_k"], lw["w_vt"], tables[0], tables[1], tables[2], halo)


def _attention_body(*refs, tile, has_meta, group):
    if has_meta:
        q_ref, k_ref, vt_ref, km_ref, vtm_ref, o_ref, s_scr, m_scr, acc_scr = refs
    else:
        q_ref, k_ref, vt_ref, o_ref, s_scr, m_scr, acc_scr = refs
    i = pl.program_id(1)
    n_groups = N_HEADS // group

    def produce(slot, hg, t, diag):
        k_rows = pl.ds(pl.multiple_of(t * tile, tile), tile)
        for g in range(group):
            hd = hg * group + g
            s = lax.dot_general(k_ref[hd, k_rows, :], q_ref[hd], _NT,
                                preferred_element_type=jnp.float32)
            if diag:
                key_idx = lax.broadcasted_iota(jnp.int32, (tile, tile), 0)
                qry_idx = lax.broadcasted_iota(jnp.int32, (tile, tile), 1)
                s = jnp.where(key_idx <= qry_idx, s, MASK_VALUE)
            s_scr[slot, g] = s

    def consume(slot, hg, t, diag):
        heads = [hg * group + g for g in range(group)]
        with_meta = diag and has_meta
        ss = [s_scr[slot, g] for g in range(group)]
        m_olds = [m_scr[hd] for hd in heads]
        m_news = [jnp.maximum(m_olds[g], jnp.max(ss[g], axis=0, keepdims=True))
                  for g in range(group)]
        if with_meta:
            sms = [lax.dot_general(km_ref[hd, 0:N_META, :], q_ref[hd], _NT,
                                   preferred_element_type=jnp.float32) for hd in heads]
            m_news = [jnp.maximum(m_news[g], jnp.max(sms[g], axis=0, keepdims=True))
                      for g in range(group)]
        ps = [jnp.exp2(ss[g] - m_news[g]).astype(jnp.bfloat16) for g in range(group)]
        ones_blk = jnp.ones((ONES_ROWS, tile), jnp.bfloat16)
        for g, hd in enumerate(heads):
            v_rows = pl.ds(pl.multiple_of(hd * V_DIM, V_DIM), V_DIM)
            pv = _dot(jnp.concatenate([vt_ref[t, v_rows, :], ones_blk], axis=0), ps[g])
            if with_meta:
                pm = jnp.exp2(sms[g] - m_news[g]).astype(jnp.bfloat16)
                ones_meta = jnp.ones((ONES_ROWS, N_META), jnp.bfloat16)
                pv = pv + _dot(jnp.concatenate([vtm_ref[0, v_rows, 0:N_META], ones_meta], axis=0), pm)
            acc_scr[hd] = jnp.exp2(m_olds[g] - m_news[g]) * acc_scr[hd] + pv
            m_scr[hd] = m_news[g]

    def key_block(t, diag, next_diag):
        for hg in range(n_groups):
            if hg + 1 < n_groups:
                produce((hg + 1) % 2, hg + 1, t, diag)
            elif next_diag is not None:
                produce(0, 0, t + 1, next_diag)
            consume(hg % 2, hg, t, diag)

    m_scr[...] = jnp.full_like(m_scr, MASK_VALUE)
    acc_scr[...] = jnp.zeros_like(acc_scr)

    @pl.when(i > 0)
    def _():
        produce(0, 0, 0, False)

    def full_block(t, carry):
        key_block(t, False, False)
        return carry
    lax.fori_loop(0, i - 1, full_block, 0)

    @pl.when(i > 0)
    def _():
        key_block(i - 1, False, True)

    @pl.when(i == 0)
    def _():
        produce(0, 0, i, True)

    key_block(i, True, None)

    for pair in range(N_HEADS // 2):
        acc0, acc1 = acc_scr[2 * pair], acc_scr[2 * pair + 1]
        both = jnp.concatenate([acc0[0:V_DIM] / acc0[V_DIM:V_DIM + 1],
                                acc1[0:V_DIM] / acc1[V_DIM:V_DIM + 1]], axis=0)
        o_ref[:, pair * LANES:(pair + 1) * LANES] = both.T.astype(jnp.bfloat16)


def _attention(q, k, vt, meta_kv, *, batch, seq, tile):
    rows = batch * seq
    nq = seq // tile
    has_meta = meta_kv is not None
    assert (N_HEADS // HEAD_GROUP) % 2 == 0
    in_specs = [
        pl.BlockSpec((N_HEADS, tile, HEAD_PAD), lambda b, i: (0, b * nq + i, 0)),
        pl.BlockSpec((N_HEADS, seq, HEAD_PAD), lambda b, i: (0, b, 0)),
        pl.BlockSpec((nq, ATT_WIDTH, tile), lambda b, i: (b, 0, 0)),
    ]
    args = [q, k, vt]
    if has_meta:
        km, vtm = meta_kv
        in_specs += [_const_spec(km.shape), _const_spec(vtm.shape)]
        args += [km, vtm]
    body = functools.partial(_attention_body, tile=tile, has_meta=has_meta, group=HEAD_GROUP)
    return pl.pallas_call(
        body,
        grid=(batch, nq),
        in_specs=in_specs,
        out_specs=pl.BlockSpec((tile, ATT_WIDTH), lambda b, i: (b * nq + i, 0)),
        out_shape=jax.ShapeDtypeStruct((rows, ATT_WIDTH), jnp.bfloat16),
        scratch_shapes=[
            pltpu.VMEM((2, HEAD_GROUP, tile, tile), jnp.float32),
            pltpu.VMEM((N_HEADS, 1, tile), jnp.float32),
            pltpu.VMEM((N_HEADS, V_DIM + ONES_ROWS, tile), jnp.float32),
        ],
        compiler_params=pltpu.CompilerParams(
            dimension_semantics=("parallel", "parallel"), vmem_limit_bytes=VMEM_LIMIT),
        name="attention",
    )(*args)


def _post_attention_body(h_ref, a_ref, b_ref, gmix_ref, wgates_ref, wpa_ref, wpb_ref, wo_ref,
                         gffn_ref, wgate_ref, wup_ref, wdown_ref, gfin_ref, o_ref, *, final_norm):
    h = h_ref[...]
    hn = _rmsnorm(h, gmix_ref[...]).astype(jnp.bfloat16)
    gates = jax.nn.sigmoid(_dot(hn, wgates_ref[...]))
    merged = (gates[:, :D_MODEL] * _dot(a_ref[...], wpa_ref[...])
              + gates[:, D_MODEL:] * _dot(b_ref[...], wpb_ref[...]))
    h = h + _dot(merged.astype(jnp.bfloat16), wo_ref[...])
    hn = _rmsnorm(h, gffn_ref[...]).astype(jnp.bfloat16)
    act = jax.nn.silu(_dot(hn, wgate_ref[...])) * _dot(hn, wup_ref[...])
    h = h + _dot(act.astype(jnp.bfloat16), wdown_ref[...])
    if final_norm:
        h = _rmsnorm(h, gfin_ref[...])
    o_ref[...] = h


def _post_attention(h, a, b, lw, g_final, *, tile, final_norm):
    rows = h.shape[0]
    row_map = lambda r: (r, 0)
    body = functools.partial(_post_attention_body, final_norm=final_norm)
    return pl.pallas_call(
        body,
        grid=(rows // tile,),
        in_specs=[
            pl.BlockSpec((tile, D_MODEL), row_map),
            pl.BlockSpec((tile, POOL_WIDTH), row_map),
            pl.BlockSpec((tile, ATT_WIDTH), row_map),
            _const_spec((1, D_MODEL)),
            _const_spec((D_MODEL, 2 * D_MODEL)),
            _const_spec((POOL_WIDTH, D_MODEL)),
            _const_spec((ATT_WIDTH, D_MODEL)),
            _const_spec((D_MODEL, D_MODEL)),
            _const_spec((1, D_MODEL)),
            _const_spec((D_MODEL, D_FF)),
            _const_spec((D_MODEL, D_FF)),
            _const_spec((D_FF, D_MODEL)),
            _const_spec((1, D_MODEL)),
        ],
        out_specs=pl.BlockSpec((tile, D_MODEL), row_map),
        out_shape=jax.ShapeDtypeStruct((rows, D_MODEL), jnp.float32),
        compiler_params=pltpu.CompilerParams(
            dimension_semantics=("parallel",), vmem_limit_bytes=VMEM_LIMIT),
        name="post_attention",
    )(h, a, b, lw["g_mix"], lw["w_in_gates"], lw["w_pa"], lw["w_pb"], lw["w_o"], lw["g_ffn"],
      lw["w_gate"], lw["w_up"], lw["w_down"], g_final)


def _layer_weights(i, norm_mix_g, w_in, pool_w, pool_scale, q_norm_g, kv_norm_g, w_uq, w_ukv,
                   w_pa, w_pb, w_o, norm_ffn_g, w_gate, w_up, w_down):
    bf = jnp.bfloat16
    o_q = POOL_WIDTH + Q_RANK
    o_kv = o_q + KV_RANK
    o_kr = o_kv + QK_ROPE
    wi = w_in[i]
    kr_cols = jnp.pad(wi[:, o_kv:o_kr], ((0, 0), (QK_NOPE, LANES - QK_DIM)))
    w_in_a = jnp.concatenate([wi[:, :o_kv], kr_cols], axis=1).astype(bf)
    w_q = jnp.pad(w_uq[i].reshape(Q_RANK, N_HEADS, QK_DIM),
                  ((0, 0), (0, 0), (0, HEAD_PAD - QK_DIM))).reshape(Q_RANK, N_HEADS * HEAD_PAD)
    ukv = w_ukv[i].reshape(KV_RANK, N_HEADS, QK_NOPE + V_DIM)
    w_k = jnp.pad(ukv[:, :, :QK_NOPE],
                  ((0, 0), (0, 0), (0, HEAD_PAD - QK_NOPE))).reshape(KV_RANK, N_HEADS * HEAD_PAD)
    w_vt = ukv[:, :, QK_NOPE:].reshape(KV_RANK, ATT_WIDTH).T
    return dict(
        g_mix=norm_mix_g[i][None], w_in_a=w_in_a, w_in_gates=wi[:, o_kr:].astype(bf),
        pool_w=pool_w[i].astype(bf), pool_scale=pool_scale[i][None],
        q_g=q_norm_g[i][None], kv_g=kv_norm_g[i][None],
        w_q=w_q.astype(bf), w_k=w_k.astype(bf), w_vt=w_vt.astype(bf),
        w_pa=w_pa[i].astype(bf), w_pb=w_pb[i].astype(bf), w_o=w_o[i].astype(bf),
        g_ffn=norm_ffn_g[i][None], w_gate=w_gate[i].astype(bf), w_up=w_up[i].astype(bf),
        w_down=w_down[i].astype(bf))


def _rope_tables(length):
    inv = 1.0 / (ROPE_THETA ** (jnp.arange(0, QK_ROPE, 2, dtype=jnp.float32) / QK_ROPE))
    ang = jnp.arange(length, dtype=jnp.float32)[:, None] * inv[None, :]
    cos, sin = jnp.cos(ang), jnp.sin(ang)
    half = QK_ROPE // 2
    zeros = jnp.zeros((length, half), jnp.float32)
    pad = jnp.zeros((length, LANES - QK_DIM), jnp.float32)
    cos_t = jnp.concatenate([jnp.ones((length, QK_NOPE), jnp.float32), cos, cos, pad], axis=1)
    sin_lo = jnp.concatenate([jnp.zeros((length, QK_NOPE), jnp.float32), -sin, zeros, pad], axis=1)
    sin_hi = jnp.concatenate([jnp.zeros((length, QK_NOPE), jnp.float32), zeros, sin, pad], axis=1)
    return cos_t, sin_lo, sin_hi


def kernel(x, meta_tokens, norm_mix_g, w_in, pool_w, pool_scale, q_norm_g, kv_norm_g, w_uq, w_ukv,
           w_pa, w_pb, w_o, norm_ffn_g, w_gate, w_up, w_down, final_norm_g):
    batch, seq, _ = x.shape
    depth = w_in.shape[0]
    tile = 512
    tables = _rope_tables(N_META + max(seq, META_TILE))
    meta_tables = tuple(t[:META_TILE] for t in tables)
    main_tables = tuple(t[N_META:N_META + seq] for t in tables)
    g_final = final_norm_g[None]

    h = x.reshape(batch * seq, D_MODEL)
    hm = jnp.pad(meta_tokens.astype(x.dtype), ((0, META_TILE - N_META), (0, 0)))
    zero_halo = jnp.zeros((MAX_WINDOW, POOL_WIDTH), jnp.float32)
    for i in range(depth):
        lw = _layer_weights(i, norm_mix_g, w_in, pool_w, pool_scale, q_norm_g, kv_norm_g, w_uq,
                            w_ukv, w_pa, w_pb, w_o, norm_ffn_g, w_gate, w_up, w_down)
        last = i == depth - 1
        qm, km, vtm, am, um = _pre_attention(hm, lw, meta_tables, zero_halo, batch=1,
                                             seq=META_TILE, tile=META_TILE, pos_offset=0,
                                             emit_u=True)
        if not last:
            bm = _attention(qm, km, vtm, None, batch=1, seq=META_TILE, tile=META_TILE)
            hm = _post_attention(hm, am, bm, lw, g_final, tile=META_TILE, final_norm=False)
        q, k, vt, a = _pre_attention(h, lw, main_tables, um, batch=batch, seq=seq, tile=tile,
                                     pos_offset=N_META, emit_u=False)
        b = _attention(q, k, vt, (km, vtm), batch=batch, seq=seq, tile=tile)
        h = _post_attention(h, a, b, lw, g_final, tile=tile, final_norm=last)
    return h.reshape(batch, seq, D_MODEL)
```

```python
import functools
import math

import jax
import jax.numpy as jnp
from jax import lax
from jax.experimental import pallas as pl
from jax.experimental.pallas import tpu as pltpu

D_MODEL = 1024
N_META = 16
POOL_WINDOWS = (2, 4, 8, 16)
POOL_GROUP = 128
POOL_WIDTH = POOL_GROUP * len(POOL_WINDOWS)
N_HEADS = 16
QK_NOPE = 64
QK_ROPE = 32
V_DIM = 64
Q_RANK = 256
KV_RANK = 128
QK_DIM = QK_NOPE + QK_ROPE
ATT_WIDTH = N_HEADS * V_DIM
ROPE_THETA = 10000.0
D_FF = 2816
NORM_EPS = 1e-6
MASK_VALUE = -1e30

LANES = 128
HEAD_PAD = LANES
MAX_WINDOW = max(POOL_WINDOWS)
POOL_PAD = 8
Q_SCALE = (QK_DIM ** -0.5) * math.log2(math.e)
META_TILE = 128
HEAD_GROUP = 2
ONES_ROWS = 16
DIAG_BLOCK = 256
VMEM_LIMIT = 56 * 1024 * 1024

_NT = (((1,), (1,)), ((), ()))


def _rmsnorm(x, g):
    return x * lax.rsqrt(jnp.mean(x * x, axis=-1, keepdims=True) + NORM_EPS) * g


def _dot(a, b):
    return jnp.dot(a, b, preferred_element_type=jnp.float32)


def _const_spec(shape):
    nd = len(shape)
    return pl.BlockSpec(shape, lambda *_: (0,) * nd, pipeline_mode=pl.Buffered(1))


def _pre_attention_body(h_ref, gmix_ref, win_ref, poolw_ref, pscale_ref, qg_ref, kvg_ref,
                        wq_ref, wk_ref, wvt_ref, cosq_ref, sinq_ref, cosk_ref, sink_ref, halo_ref,
                        q_out, k_out, vt_out, a_out, *rest, tile, pos_offset, emit_u):
    if emit_u:
        u_out, pool_scr = rest
    else:
        (pool_scr,) = rest
    j = pl.program_id(1)
    row0 = pl.multiple_of(j * tile, tile)
    rows = pl.ds(row0, tile)

    hn = _rmsnorm(h_ref[...], gmix_ref[...]).astype(jnp.bfloat16)
    z = _dot(hn, win_ref[...])
    u = z[:, :POOL_WIDTH]
    c_q = z[:, POOL_WIDTH:POOL_WIDTH + Q_RANK]
    c_kv = z[:, POOL_WIDTH + Q_RANK:POOL_WIDTH + Q_RANK + KV_RANK]
    kr = z[:, POOL_WIDTH + Q_RANK + KV_RANK:]

    def rope(x, cos, sin):
        return x * cos + pltpu.roll(x, LANES // 2, 1) * sin

    cqn = _rmsnorm(c_q, qg_ref[...]).astype(jnp.bfloat16)
    q_raw = _dot(cqn, wq_ref[...])
    cos_q, sin_q = cosq_ref[rows, :], sinq_ref[rows, :]
    for hd in range(N_HEADS):
        x = q_raw[:, hd * HEAD_PAD:(hd + 1) * HEAD_PAD]
        q_out[hd] = rope(x, cos_q, sin_q).astype(jnp.bfloat16)

    ckvn = _rmsnorm(c_kv, kvg_ref[...]).astype(jnp.bfloat16)
    k_raw = _dot(ckvn, wk_ref[...])
    kr_roped = rope(kr, cosk_ref[rows, :], sink_ref[rows, :])
    for hd in range(N_HEADS):
        k_out[hd] = (k_raw[:, hd * HEAD_PAD:(hd + 1) * HEAD_PAD] + kr_roped).astype(jnp.bfloat16)
    vt = lax.dot_general(wvt_ref[...], ckvn, _NT, preferred_element_type=jnp.float32)
    vt_out[0] = vt.astype(jnp.bfloat16)

    first, end = POOL_PAD + MAX_WINDOW, POOL_PAD + MAX_WINDOW + tile

    @pl.when(j == 0)
    def _():
        pool_scr[:, 0:POOL_PAD, :] = jnp.zeros((3, POOL_PAD, POOL_WIDTH), jnp.float32)
        pool_scr[0, POOL_PAD:first, :] = halo_ref[...]
    pool_scr[0, first:end, :] = u
    if emit_u:
        u_out[...] = u
    g1, g2, g3 = POOL_GROUP, 2 * POOL_GROUP, 3 * POOL_GROUP
    pool_scr[1, POOL_PAD:end, :] = pool_scr[0, POOL_PAD:end, :] + pool_scr[0, POOL_PAD - 1:end - 1, :]
    pool_scr[2, POOL_PAD:end, g1:] = (pool_scr[1, POOL_PAD:end, g1:]
                                      + pool_scr[1, POOL_PAD - 2:end - 2, g1:])
    pool_scr[1, POOL_PAD:end, g2:] = (pool_scr[2, POOL_PAD:end, g2:]
                                      + pool_scr[2, POOL_PAD - 4:end - 4, g2:])
    wsums = [pool_scr[1, first:end, 0:g1], pool_scr[2, first:end, g1:g2],
             pool_scr[1, first:end, g2:g3],
             pool_scr[1, first:end, g3:] + pool_scr[1, first - 8:end - 8, g3:]]
    pos = pos_offset + row0 + lax.broadcasted_iota(jnp.int32, (tile, POOL_GROUP), 0)
    outs = []
    for g, w in enumerate(POOL_WINDOWS):
        count = jnp.minimum(pos + 1, w).astype(jnp.float32)
        y = (wsums[g] / count - u[:, g * POOL_GROUP:(g + 1) * POOL_GROUP]).astype(jnp.bfloat16)
        outs.append(_dot(y, poolw_ref[g]))
    a = jnp.concatenate(outs, axis=1) * pscale_ref[...]
    a_out[...] = a.astype(jnp.bfloat16)
    pool_scr[0, POOL_PAD:first, :] = pool_scr[0, end - MAX_WINDOW:end, :]


def _pre_attention(h, lw, tables, halo, *, batch, seq, tile, pos_offset, emit_u):
    rows = batch * seq
    n_tiles = seq // tile
    row_map = lambda b, j: (b * n_tiles + j, 0)
    in_specs = [
        pl.BlockSpec((tile, D_MODEL), row_map),
        _const_spec((1, D_MODEL)),
        _const_spec((D_MODEL, 1024)),
        _const_spec((len(POOL_WINDOWS), POOL_GROUP, POOL_GROUP)),
        _const_spec((1, POOL_WIDTH)),
        _const_spec((1, Q_RANK)),
        _const_spec((1, KV_RANK)),
        _const_spec((Q_RANK, N_HEADS * HEAD_PAD)),
        _const_spec((KV_RANK, N_HEADS * HEAD_PAD)),
        _const_spec((ATT_WIDTH, KV_RANK)),
        _const_spec((seq, LANES)),
        _const_spec((seq, LANES)),
        _const_spec((seq, LANES)),
        _const_spec((seq, LANES)),
        pl.BlockSpec((MAX_WINDOW, POOL_WIDTH), lambda b, j: (0, 0)),
    ]
    out_shape = [
        jax.ShapeDtypeStruct((N_HEADS, rows, HEAD_PAD), jnp.bfloat16),
        jax.ShapeDtypeStruct((N_HEADS, rows, HEAD_PAD), jnp.bfloat16),
        jax.ShapeDtypeStruct((rows // tile, ATT_WIDTH, tile), jnp.bfloat16),
        jax.ShapeDtypeStruct((rows, POOL_WIDTH), jnp.bfloat16),
    ]
    out_specs = [
        pl.BlockSpec((N_HEADS, tile, HEAD_PAD), lambda b, j: (0, b * n_tiles + j, 0)),
        pl.BlockSpec((N_HEADS, tile, HEAD_PAD), lambda b, j: (0, b * n_tiles + j, 0)),
        pl.BlockSpec((1, ATT_WIDTH, tile), lambda b, j: (b * n_tiles + j, 0, 0)),
        pl.BlockSpec((tile, POOL_WIDTH), row_map),
    ]
    if emit_u:
        out_shape.append(jax.ShapeDtypeStruct((rows, POOL_WIDTH), jnp.float32))
        out_specs.append(pl.BlockSpec((tile, POOL_WIDTH), row_map))
    body = functools.partial(_pre_attention_body, tile=tile, pos_offset=pos_offset, emit_u=emit_u)
    return pl.pallas_call(
        body,
        grid=(batch, n_tiles),
        in_specs=in_specs,
        out_specs=out_specs,
        out_shape=out_shape,
        scratch_shapes=[pltpu.VMEM((3, POOL_PAD + MAX_WINDOW + tile, POOL_WIDTH), jnp.float32)],
        compiler_params=pltpu.CompilerParams(
            dimension_semantics=("arbitrary", "arbitrary"), vmem_limit_bytes=VMEM_LIMIT),
        name="pre_attention",
    )(h, lw["g_mix"], lw["w_in_a"], lw["pool_w"], lw["pool_scale"], lw["q_g"], lw["kv_g"],
      lw["w_q"], lw["w_k"], lw["w_vt"], *tables, halo)


def _attention_body(*refs, tile, diag, has_meta, group):
    if has_meta:
        q_ref, k_ref, vt_ref, tri_ref, km_ref, vtm_ref, o_ref, s_scr, smax_scr, m_scr, acc_scr = refs
    else:
        q_ref, k_ref, vt_ref, tri_ref, o_ref, s_scr, smax_scr, m_scr, acc_scr = refs
    i = pl.program_id(1)
    n_groups = N_HEADS // group
    n_sub = tile // diag

    def produce_full(slot, hg, t):
        k_rows = pl.ds(pl.multiple_of(t * tile, tile), tile)
        for g in range(group):
            hd = hg * group + g
            s = lax.dot_general(k_ref[hd, k_rows, :], q_ref[hd], _NT,
                                preferred_element_type=jnp.float32)
            s_scr[slot, g, 0:tile, :] = s
            smax_scr[slot, g] = jnp.max(s, axis=0, keepdims=True)

    def produce_diag(slot, hg, t):
        for g in range(group):
            hd = hg * group + g
            q = q_ref[hd]
            smax = None
            for c in range(n_sub):
                lo = c * diag
                k = k_ref[hd, pl.ds(pl.multiple_of(t * tile + lo, diag), diag), :]
                if c == 0 and has_meta:
                    k = jnp.concatenate([k, km_ref[hd, 0:N_META, :]], axis=0)
                s = lax.dot_general(k, q[lo:, :], _NT, preferred_element_type=jnp.float32)
                on_diag = s[0:diag, 0:diag] + tri_ref[...]
                main = on_diag if lo + diag == tile else jnp.concatenate(
                    [on_diag, s[0:diag, diag:]], axis=1)
                s_scr[slot, g, lo:lo + diag, lo:] = main
                cmax = jnp.max(main, axis=0, keepdims=True)
                if c == 0 and has_meta:
                    s_scr[slot, g, tile:tile + N_META, :] = s[diag:, :]
                    cmax = jnp.maximum(cmax, jnp.max(s[diag:, :], axis=0, keepdims=True))
                smax = cmax if c == 0 else jnp.concatenate(
                    [smax[:, :lo], jnp.maximum(smax[:, lo:], cmax)], axis=1)
            smax_scr[slot, g] = smax

    def consume(slot, hg, t, is_diag):
        for g in range(group):
            hd = hg * group + g
            v_rows = pl.ds(pl.multiple_of(hd * V_DIM, V_DIM), V_DIM)
            m_old = m_scr[hd]
            m_new = jnp.maximum(m_old, smax_scr[slot, g])
            if not is_diag:
                p = jnp.exp2(s_scr[slot, g, 0:tile, :] - m_new).astype(jnp.bfloat16)
                ones = jnp.ones((ONES_ROWS, tile), jnp.bfloat16)
                pv = _dot(jnp.concatenate([vt_ref[t, v_rows, :], ones], axis=0), p)
            else:
                ones = jnp.ones((ONES_ROWS, diag), jnp.bfloat16)
                for c in range(n_sub):
                    lo = c * diag
                    p = jnp.exp2(s_scr[slot, g, lo:lo + diag, lo:] - m_new[:, lo:]).astype(jnp.bfloat16)
                    part = _dot(jnp.concatenate([vt_ref[t, v_rows, lo:lo + diag], ones], axis=0), p)
                    pv = part if c == 0 else jnp.concatenate(
                        [pv[:, :lo], pv[:, lo:] + part], axis=1)
                if has_meta:
                    pm = jnp.exp2(s_scr[slot, g, tile:tile + N_META, :] - m_new).astype(jnp.bfloat16)
                    ones_meta = jnp.ones((ONES_ROWS, N_META), jnp.bfloat16)
                    pv = pv + _dot(
                        jnp.concatenate([vtm_ref[0, v_rows, 0:N_META], ones_meta], axis=0), pm)
            acc_scr[hd] = jnp.exp2(m_old - m_new) * acc_scr[hd] + pv
            m_scr[hd] = m_new

    def key_block(t, is_diag, next_diag):
        for hg in range(n_groups):
            if hg + 1 < n_groups:
                (produce_diag if is_diag else produce_full)((hg + 1) % 2, hg + 1, t)
            elif next_diag is not None:
                (produce_diag if next_diag else produce_full)(0, 0, t + 1)
            consume(hg % 2, hg, t, is_diag)

    m_scr[...] = jnp.full_like(m_scr, MASK_VALUE)
    acc_scr[...] = jnp.zeros_like(acc_scr)

    @pl.when(i > 0)
    def _():
        produce_full(0, 0, 0)

    def full_block(t, carry):
        key_block(t, False, False)
        return carry
    lax.fori_loop(0, i - 1, full_block, 0)

    @pl.when(i > 0)
    def _():
        key_block(i - 1, False, True)

    @pl.when(i == 0)
    def _():
        produce_diag(0, 0, i)

    key_block(i, True, None)

    for pair in range(N_HEADS // 2):
        acc0, acc1 = acc_scr[2 * pair], acc_scr[2 * pair + 1]
        both = jnp.concatenate([acc0[0:V_DIM] / acc0[V_DIM:V_DIM + 1],
                                acc1[0:V_DIM] / acc1[V_DIM:V_DIM + 1]], axis=0)
        o_ref[:, pair * LANES:(pair + 1) * LANES] = both.T.astype(jnp.bfloat16)


def _attention(q, k, vt, meta_kv, *, batch, seq, tile):
    rows = batch * seq
    nq = seq // tile
    has_meta = meta_kv is not None
    assert (N_HEADS // HEAD_GROUP) % 2 == 0
    diag = min(tile, DIAG_BLOCK)
    idx = jnp.arange(diag, dtype=jnp.int32)
    tri = jnp.where(idx[:, None] <= idx[None, :], 0.0, MASK_VALUE).astype(jnp.float32)
    in_specs = [
        pl.BlockSpec((N_HEADS, tile, HEAD_PAD), lambda b, i: (0, b * nq + i, 0)),
        pl.BlockSpec((N_HEADS, seq, HEAD_PAD), lambda b, i: (0, b, 0)),
        pl.BlockSpec((nq, ATT_WIDTH, tile), lambda b, i: (b, 0, 0)),
        _const_spec((diag, diag)),
    ]
    args = [q, k, vt, tri]
    if has_meta:
        km, vtm = meta_kv
        in_specs += [_const_spec(km.shape), _const_spec(vtm.shape)]
        args += [km, vtm]
    body = functools.partial(_attention_body, tile=tile, diag=diag, has_meta=has_meta,
                             group=HEAD_GROUP)
    return pl.pallas_call(
        body,
        grid=(batch, nq),
        in_specs=in_specs,
        out_specs=pl.BlockSpec((tile, ATT_WIDTH), lambda b, i: (b * nq + i, 0)),
        out_shape=jax.ShapeDtypeStruct((rows, ATT_WIDTH), jnp.bfloat16),
        scratch_shapes=[
            pltpu.VMEM((2, HEAD_GROUP, tile + N_META, tile), jnp.float32),
            pltpu.VMEM((2, HEAD_GROUP, 1, tile), jnp.float32),
            pltpu.VMEM((N_HEADS, 1, tile), jnp.float32),
            pltpu.VMEM((N_HEADS, V_DIM + ONES_ROWS, tile), jnp.float32),
        ],
        compiler_params=pltpu.CompilerParams(
            dimension_semantics=("parallel", "parallel"), vmem_limit_bytes=VMEM_LIMIT),
        name="attention",
    )(*args)


def _post_attention_body(h_ref, a_ref, b_ref, gmix_ref, wgates_ref, wpa_ref, wpb_ref, wo_ref,
                         gffn_ref, wgate_ref, wup_ref, wdown_ref, gfin_ref, o_ref, *, final_norm):
    h = h_ref[...]
    hn = _rmsnorm(h, gmix_ref[...]).astype(jnp.bfloat16)
    gates = jax.nn.sigmoid(_dot(hn, wgates_ref[...]))
    merged = (gates[:, :D_MODEL] * _dot(a_ref[...], wpa_ref[...])
              + gates[:, D_MODEL:] * _dot(b_ref[...], wpb_ref[...]))
    h = h + _dot(merged.astype(jnp.bfloat16), wo_ref[...])
    hn = _rmsnorm(h, gffn_ref[...]).astype(jnp.bfloat16)
    act = jax.nn.silu(_dot(hn, wgate_ref[...])) * _dot(hn, wup_ref[...])
    h = h + _dot(act.astype(jnp.bfloat16), wdown_ref[...])
    if final_norm:
        h = _rmsnorm(h, gfin_ref[...])
    o_ref[...] = h


def _post_attention(h, a, b, lw, g_final, *, tile, final_norm):
    rows = h.shape[0]
    row_map = lambda r: (r, 0)
    body = functools.partial(_post_attention_body, final_norm=final_norm)
    return pl.pallas_call(
        body,
        grid=(rows // tile,),
        in_specs=[
            pl.BlockSpec((tile, D_MODEL), row_map),
            pl.BlockSpec((tile, POOL_WIDTH), row_map),
            pl.BlockSpec((tile, ATT_WIDTH), row_map),
            _const_spec((1, D_MODEL)),
            _const_spec((D_MODEL, 2 * D_MODEL)),
            _const_spec((POOL_WIDTH, D_MODEL)),
            _const_spec((ATT_WIDTH, D_MODEL)),
            _const_spec((D_MODEL, D_MODEL)),
            _const_spec((1, D_MODEL)),
            _const_spec((D_MODEL, D_FF)),
            _const_spec((D_MODEL, D_FF)),
            _const_spec((D_FF, D_MODEL)),
            _const_spec((1, D_MODEL)),
        ],
        out_specs=pl.BlockSpec((tile, D_MODEL), row_map),
        out_shape=jax.ShapeDtypeStruct((rows, D_MODEL), jnp.float32),
        compiler_params=pltpu.CompilerParams(
            dimension_semantics=("parallel",), vmem_limit_bytes=VMEM_LIMIT),
        name="post_attention",
    )(h, a, b, lw["g_mix"], lw["w_in_gates"], lw["w_pa"], lw["w_pb"], lw["w_o"], lw["g_ffn"],
      lw["w_gate"], lw["w_up"], lw["w_down"], g_final)


def _head_lanes():
    half = QK_ROPE // 2
    split = LANES // 2 - half
    return [(QK_NOPE, QK_NOPE + half), (0, split), (QK_NOPE + half, QK_DIM), (split, QK_NOPE),
            (None, HEAD_PAD - QK_DIM)]


def _place_head_columns(w, keep):
    parts = []
    for start, stop in _head_lanes():
        if start is None or not keep(start):
            width = stop if start is None else stop - start
            parts.append(jnp.zeros(w.shape[:-1] + (width,), w.dtype))
        else:
            parts.append(w[..., start:stop])
    return jnp.concatenate(parts, axis=-1)


def _layer_weights(i, norm_mix_g, w_in, pool_w, pool_scale, q_norm_g, kv_norm_g, w_uq, w_ukv,
                   w_pa, w_pb, w_o, norm_ffn_g, w_gate, w_up, w_down):
    bf = jnp.bfloat16
    o_q = POOL_WIDTH + Q_RANK
    o_kv = o_q + KV_RANK
    o_kr = o_kv + QK_ROPE
    is_nope = lambda col: col < QK_NOPE
    wi = w_in[i]
    kr_cols = _place_head_columns(jnp.pad(wi[:, o_kv:o_kr], ((0, 0), (QK_NOPE, 0))),
                                  lambda col: not is_nope(col))
    w_in_a = jnp.concatenate([wi[:, :o_kv], kr_cols], axis=1).astype(bf)
    w_q = _place_head_columns(w_uq[i].reshape(Q_RANK, N_HEADS, QK_DIM), lambda col: True)
    ukv = w_ukv[i].reshape(KV_RANK, N_HEADS, QK_NOPE + V_DIM)
    w_k = _place_head_columns(jnp.pad(ukv[:, :, :QK_NOPE], ((0, 0), (0, 0), (0, QK_ROPE))), is_nope)
    w_vt = ukv[:, :, QK_NOPE:].reshape(KV_RANK, ATT_WIDTH).T
    return dict(
        g_mix=norm_mix_g[i][None], w_in_a=w_in_a, w_in_gates=wi[:, o_kr:].astype(bf),
        pool_w=pool_w[i].astype(bf), pool_scale=pool_scale[i][None],
        q_g=q_norm_g[i][None], kv_g=kv_norm_g[i][None],
        w_q=w_q.reshape(Q_RANK, N_HEADS * HEAD_PAD).astype(bf),
        w_k=w_k.reshape(KV_RANK, N_HEADS * HEAD_PAD).astype(bf), w_vt=w_vt.astype(bf),
        w_pa=w_pa[i].astype(bf), w_pb=w_pb[i].astype(bf), w_o=w_o[i].astype(bf),
        g_ffn=norm_ffn_g[i][None], w_gate=w_gate[i].astype(bf), w_up=w_up[i].astype(bf),
        w_down=w_down[i].astype(bf))


def _rope_tables(length):
    inv = 1.0 / (ROPE_THETA ** (jnp.arange(0, QK_ROPE, 2, dtype=jnp.float32) / QK_ROPE))
    ang = jnp.arange(length, dtype=jnp.float32)[:, None] * inv[None, :]
    cos, sin = jnp.cos(ang), jnp.sin(ang)
    ones = jnp.ones((length, QK_NOPE), jnp.float32)
    cos_t = _place_head_columns(jnp.concatenate([ones, cos, cos], axis=1), lambda col: True)
    sin_t = _place_head_columns(jnp.concatenate([0.0 * ones, -sin, sin], axis=1), lambda col: True)
    return cos_t * Q_SCALE, sin_t * Q_SCALE, cos_t, sin_t


def kernel(x, meta_tokens, norm_mix_g, w_in, pool_w, pool_scale, q_norm_g, kv_norm_g, w_uq, w_ukv,
           w_pa, w_pb, w_o, norm_ffn_g, w_gate, w_up, w_down, final_norm_g):
    batch, seq, _ = x.shape
    depth = w_in.shape[0]
    tile = 512
    tables = _rope_tables(N_META + max(seq, META_TILE))
    meta_tables = tuple(t[:META_TILE] for t in tables)
    main_tables = tuple(t[N_META:N_META + seq] for t in tables)
    g_final = final_norm_g[None]

    h = x.reshape(batch * seq, D_MODEL)
    hm = jnp.pad(meta_tokens.astype(x.dtype), ((0, META_TILE - N_META), (0, 0)))
    zero_halo = jnp.zeros((MAX_WINDOW, POOL_WIDTH), jnp.float32)
    for i in range(depth):
        lw = _layer_weights(i, norm_mix_g, w_in, pool_w, pool_scale, q_norm_g, kv_norm_g, w_uq,
                            w_ukv, w_pa, w_pb, w_o, norm_ffn_g, w_gate, w_up, w_down)
        last = i == depth - 1
        qm, km, vtm, am, um = _pre_attention(hm, lw, meta_tables, zero_halo, batch=1,
                                             seq=META_TILE, tile=META_TILE, pos_offset=0,
                                             emit_u=True)
        if not last:
            bm = _attention(qm, km, vtm, None, batch=1, seq=META_TILE, tile=META_TILE)
            hm = _post_attention(hm, am, bm, lw, g_final, tile=META_TILE, final_norm=False)
        q, k, vt, a = _pre_attention(h, lw, main_tables, um, batch=batch, seq=seq, tile=tile,
                                     pos_offset=N_META, emit_u=False)
        b = _attention(q, k, vt, (km, vtm), batch=batch, seq=seq, tile=tile)
        h = _post_attention(h, a, b, lw, g_final, tile=tile, final_norm=last)
    return h.reshape(batch, seq, D_MODEL)
```

```python
import functools
import math

import jax
import jax.numpy as jnp
import numpy as np
from jax import lax
from jax.experimental import pallas as pl
from jax.experimental.pallas import tpu as pltpu

D_MODEL = 1024
N_META = 16
POOL_WINDOWS = (2, 4, 8, 16)
POOL_GROUP = 128
POOL_WIDTH = POOL_GROUP * len(POOL_WINDOWS)
N_HEADS = 16
QK_NOPE = 64
QK_ROPE = 32
V_DIM = 64
Q_RANK = 256
KV_RANK = 128
QK_DIM = QK_NOPE + QK_ROPE
ATT_WIDTH = N_HEADS * V_DIM
ROPE_THETA = 10000.0
D_FF = 2816
NORM_EPS = 1e-6
MASK_VALUE = -1e30

LANES = 128
HEAD_PAD = LANES
MAX_WINDOW = max(POOL_WINDOWS)
POOL_PAD = 8
Q_SCALE = (QK_DIM ** -0.5) * math.log2(math.e)
META_TILE = 128
HEAD_GROUP = 1
ONES_ROWS = 16
DIAG_BLOCK = 256
VMEM_LIMIT = 56 * 1024 * 1024

_NT = (((1,), (1,)), ((), ()))


def _rmsnorm(x, g):
    return x * lax.rsqrt(jnp.mean(x * x, axis=-1, keepdims=True) + NORM_EPS) * g


def _dot(a, b):
    return jnp.dot(a, b, preferred_element_type=jnp.float32)


def _const_spec(shape, layer=None):
    zeros = (0,) * len(shape)
    if layer is None:
        return pl.BlockSpec(shape, lambda *_: zeros, pipeline_mode=pl.Buffered(1))
    return pl.BlockSpec((None,) + tuple(shape), lambda *_: (layer,) + zeros,
                        pipeline_mode=pl.Buffered(1))


def _pre_attention_body(h_ref, gmix_ref, win_ref, poolw_ref, pscale_ref, qg_ref, kvg_ref,
                        wq_ref, wk_ref, wvt_ref, cosq_ref, sinq_ref, cosk_ref, sink_ref, halo_ref,
                        q_out, k_out, vt_out, a_out, *rest, tile, pos_offset, emit_u):
    if emit_u:
        u_out, pool_scr = rest
    else:
        (pool_scr,) = rest
    j = pl.program_id(1)
    row0 = pl.multiple_of(j * tile, tile)
    rows = pl.ds(row0, tile)
    first, end = POOL_PAD + MAX_WINDOW, POOL_PAD + MAX_WINDOW + tile

    @pl.when(j == 0)
    def _():
        pool_scr[:, 0:POOL_PAD, :] = jnp.zeros((3, POOL_PAD, POOL_WIDTH), jnp.float32)
        pool_scr[0, POOL_PAD:first, :] = halo_ref[...]

    hn = _rmsnorm(h_ref[...], gmix_ref[...]).astype(jnp.bfloat16)
    z = _dot(hn, win_ref[:, POOL_WIDTH:])
    u = _dot(hn, win_ref[:, 0:POOL_WIDTH])

    c_q = z[:, 0:Q_RANK]
    c_kv = z[:, Q_RANK:Q_RANK + KV_RANK]
    kr = z[:, Q_RANK + KV_RANK:]

    def rope(x, cos, sin):
        return x * cos + pltpu.roll(x, LANES // 2, 1) * sin

    cqn = _rmsnorm(c_q, qg_ref[...]).astype(jnp.bfloat16)
    q_raw = _dot(cqn, wq_ref[...])
    cos_q, sin_q = cosq_ref[rows, :], sinq_ref[rows, :]
    for hd in range(N_HEADS):
        x = q_raw[:, hd * HEAD_PAD:(hd + 1) * HEAD_PAD]
        q_out[hd] = rope(x, cos_q, sin_q).astype(jnp.bfloat16)

    ckvn = _rmsnorm(c_kv, kvg_ref[...]).astype(jnp.bfloat16)
    k_raw = _dot(ckvn, wk_ref[...])
    kr_roped = rope(kr, cosk_ref[rows, :], sink_ref[rows, :])
    for hd in range(N_HEADS):
        k_out[hd] = (k_raw[:, hd * HEAD_PAD:(hd + 1) * HEAD_PAD] + kr_roped).astype(jnp.bfloat16)
    vt = lax.dot_general(wvt_ref[...], ckvn, _NT, preferred_element_type=jnp.float32)
    vt_out[0] = vt.astype(jnp.bfloat16)

    pool_scr[0, first:end, :] = u
    if emit_u:
        u_out[...] = u
    g1, g2, g3 = POOL_GROUP, 2 * POOL_GROUP, 3 * POOL_GROUP
    pool_scr[1, POOL_PAD:end, :] = pool_scr[0, POOL_PAD:end, :] + pool_scr[0, POOL_PAD - 1:end - 1, :]
    pool_scr[2, POOL_PAD:end, g1:] = (pool_scr[1, POOL_PAD:end, g1:]
                                      + pool_scr[1, POOL_PAD - 2:end - 2, g1:])
    pool_scr[1, POOL_PAD:end, g2:] = (pool_scr[2, POOL_PAD:end, g2:]
                                      + pool_scr[2, POOL_PAD - 4:end - 4, g2:])
    wsums = [pool_scr[1, first:end, 0:g1], pool_scr[2, first:end, g1:g2],
             pool_scr[1, first:end, g2:g3],
             pool_scr[1, first:end, g3:] + pool_scr[1, first - 8:end - 8, g3:]]
    pos = pos_offset + row0 + lax.broadcasted_iota(jnp.int32, (tile, POOL_GROUP), 0)
    outs = []
    for g, w in enumerate(POOL_WINDOWS):
        count = jnp.minimum(pos + 1, w).astype(jnp.float32)
        y = (wsums[g] / count - u[:, g * POOL_GROUP:(g + 1) * POOL_GROUP]).astype(jnp.bfloat16)
        outs.append(_dot(y, poolw_ref[g]))
    a = jnp.concatenate(outs, axis=1) * pscale_ref[...]
    a_out[...] = a.astype(jnp.bfloat16)
    pool_scr[0, POOL_PAD:first, :] = pool_scr[0, end - MAX_WINDOW:end, :]


def _pre_attention(h, lw, layer, tables, halo, *, batch, seq, tile, pos_offset, emit_u):
    rows = batch * seq
    n_tiles = seq // tile
    row_map = lambda b, j: (b * n_tiles + j, 0)
    in_specs = [
        pl.BlockSpec((tile, D_MODEL), row_map),
        _const_spec((1, D_MODEL), layer),
        _const_spec((D_MODEL, 1024), layer),
        _const_spec((len(POOL_WINDOWS), POOL_GROUP, POOL_GROUP), layer),
        _const_spec((1, POOL_WIDTH), layer),
        _const_spec((1, Q_RANK), layer),
        _const_spec((1, KV_RANK), layer),
        _const_spec((Q_RANK, N_HEADS * HEAD_PAD), layer),
        _const_spec((KV_RANK, N_HEADS * HEAD_PAD), layer),
        _const_spec((ATT_WIDTH, KV_RANK), layer),
        _const_spec((seq, LANES)),
        _const_spec((seq, LANES)),
        _const_spec((seq, LANES)),
        _const_spec((seq, LANES)),
        pl.BlockSpec((MAX_WINDOW, POOL_WIDTH), lambda b, j: (0, 0)),
    ]
    out_shape = [
        jax.ShapeDtypeStruct((N_HEADS, rows, HEAD_PAD), jnp.bfloat16),
        jax.ShapeDtypeStruct((N_HEADS, rows, HEAD_PAD), jnp.bfloat16),
        jax.ShapeDtypeStruct((rows // tile, ATT_WIDTH, tile), jnp.bfloat16),
        jax.ShapeDtypeStruct((rows, POOL_WIDTH), jnp.bfloat16),
    ]
    out_specs = [
        pl.BlockSpec((N_HEADS, tile, HEAD_PAD), lambda b, j: (0, b * n_tiles + j, 0)),
        pl.BlockSpec((N_HEADS, tile, HEAD_PAD), lambda b, j: (0, b * n_tiles + j, 0)),
        pl.BlockSpec((1, ATT_WIDTH, tile), lambda b, j: (b * n_tiles + j, 0, 0)),
        pl.BlockSpec((tile, POOL_WIDTH), row_map),
    ]
    if emit_u:
        out_shape.append(jax.ShapeDtypeStruct((rows, POOL_WIDTH), jnp.float32))
        out_specs.append(pl.BlockSpec((tile, POOL_WIDTH), row_map))
    body = functools.partial(_pre_attention_body, tile=tile, pos_offset=pos_offset, emit_u=emit_u)
    return pl.pallas_call(
        body,
        grid=(batch, n_tiles),
        in_specs=in_specs,
        out_specs=out_specs,
        out_shape=out_shape,
        scratch_shapes=[pltpu.VMEM((3, POOL_PAD + MAX_WINDOW + tile, POOL_WIDTH), jnp.float32)],
        compiler_params=pltpu.CompilerParams(
            dimension_semantics=("arbitrary", "arbitrary"), vmem_limit_bytes=VMEM_LIMIT),
        name="pre_attention",
    )(h, lw["g_mix"], lw["w_in_a"], lw["pool_w"], lw["pool_scale"], lw["q_g"], lw["kv_g"],
      lw["w_q"], lw["w_k"], lw["w_vt"], *tables, halo)


def _attention_body(*refs, tile, diag, has_meta, group):
    if has_meta:
        q_ref, k_ref, vt_ref, tri_ref, km_ref, vtm_ref, o_ref, s_scr, smax_scr, m_scr, acc_scr = refs
    else:
        q_ref, k_ref, vt_ref, tri_ref, o_ref, s_scr, smax_scr, m_scr, acc_scr = refs
    i = pl.program_id(1)
    n_groups = N_HEADS // group
    n_sub = tile // diag

    def produce_full(slot, hg, t):
        k_rows = pl.ds(pl.multiple_of(t * tile, tile), tile)
        for g in range(group):
            hd = hg * group + g
            s = lax.dot_general(k_ref[hd, k_rows, :], q_ref[hd], _NT,
                                preferred_element_type=jnp.float32)
            s_scr[slot, g, 0:tile, :] = s
            smax_scr[slot, g] = jnp.max(s, axis=0, keepdims=True)

    def produce_diag(slot, hg, t):
        for g in range(group):
            hd = hg * group + g
            q = q_ref[hd]
            smax = None
            for c in range(n_sub):
                lo = c * diag
                k = k_ref[hd, pl.ds(pl.multiple_of(t * tile + lo, diag), diag), :]
                if c == 0 and has_meta:
                    k = jnp.concatenate([k, km_ref[hd, 0:N_META, :]], axis=0)
                s = lax.dot_general(k, q[lo:, :], _NT, preferred_element_type=jnp.float32)
                on_diag = s[0:diag, 0:diag] + tri_ref[...]
                main = on_diag if lo + diag == tile else jnp.concatenate(
                    [on_diag, s[0:diag, diag:]], axis=1)
                s_scr[slot, g, lo:lo + diag, lo:] = main
                cmax = jnp.max(main, axis=0, keepdims=True)
                if c == 0 and has_meta:
                    s_scr[slot, g, tile:tile + N_META, :] = s[diag:, :]
                    cmax = jnp.maximum(cmax, jnp.max(s[diag:, :], axis=0, keepdims=True))
                smax = cmax if c == 0 else jnp.concatenate(
                    [smax[:, :lo], jnp.maximum(smax[:, lo:], cmax)], axis=1)
            smax_scr[slot, g] = smax

    def consume(slot, hg, t, is_diag):
        for g in range(group):
            hd = hg * group + g
            v_rows = pl.ds(pl.multiple_of(hd * V_DIM, V_DIM), V_DIM)
            m_old = m_scr[hd]
            m_new = jnp.maximum(m_old, smax_scr[slot, g])
            if not is_diag:
                p = jnp.exp2(s_scr[slot, g, 0:tile, :] - m_new).astype(jnp.bfloat16)
                ones = jnp.ones((ONES_ROWS, tile), jnp.bfloat16)
                pv = _dot(jnp.concatenate([vt_ref[t, v_rows, :], ones], axis=0), p)
            else:
                ones = jnp.ones((ONES_ROWS, diag), jnp.bfloat16)
                for c in range(n_sub):
                    lo = c * diag
                    p = jnp.exp2(s_scr[slot, g, lo:lo + diag, lo:] - m_new[:, lo:]).astype(jnp.bfloat16)
                    part = _dot(jnp.concatenate([vt_ref[t, v_rows, lo:lo + diag], ones], axis=0), p)
                    pv = part if c == 0 else jnp.concatenate(
                        [pv[:, :lo], pv[:, lo:] + part], axis=1)
                if has_meta:
                    pm = jnp.exp2(s_scr[slot, g, tile:tile + N_META, :] - m_new).astype(jnp.bfloat16)
                    ones_meta = jnp.ones((ONES_ROWS, N_META), jnp.bfloat16)
                    pv = pv + _dot(
                        jnp.concatenate([vtm_ref[0, v_rows, 0:N_META], ones_meta], axis=0), pm)
            acc_scr[hd] = jnp.exp2(m_old - m_new) * acc_scr[hd] + pv
            m_scr[hd] = m_new

    def key_block(t, is_diag, next_diag):
        for hg in range(n_groups):
            if hg + 1 < n_groups:
                (produce_diag if is_diag else produce_full)((hg + 1) % 2, hg + 1, t)
            elif next_diag is not None:
                (produce_diag if next_diag else produce_full)(0, 0, t + 1)
            consume(hg % 2, hg, t, is_diag)

    m_scr[...] = jnp.full_like(m_scr, MASK_VALUE)
    acc_scr[...] = jnp.zeros_like(acc_scr)

    @pl.when(i > 0)
    def _():
        produce_full(0, 0, 0)

    def full_block(t, carry):
        key_block(t, False, False)
        return carry
    lax.fori_loop(0, i - 1, full_block, 0)

    @pl.when(i > 0)
    def _():
        key_block(i - 1, False, True)

    @pl.when(i == 0)
    def _():
        produce_diag(0, 0, i)

    key_block(i, True, None)

    for pair in range(N_HEADS // 2):
        acc0, acc1 = acc_scr[2 * pair], acc_scr[2 * pair + 1]
        both = jnp.concatenate([acc0[0:V_DIM] / acc0[V_DIM:V_DIM + 1],
                                acc1[0:V_DIM] / acc1[V_DIM:V_DIM + 1]], axis=0)
        o_ref[:, pair * LANES:(pair + 1) * LANES] = both.T.astype(jnp.bfloat16)


def _attention(q, k, vt, meta_kv, *, batch, seq, tile):
    rows = batch * seq
    nq = seq // tile
    has_meta = meta_kv is not None
    assert (N_HEADS // HEAD_GROUP) % 2 == 0
    diag = min(tile, DIAG_BLOCK)
    idx = np.arange(diag)
    tri = np.where(idx[:, None] <= idx[None, :], 0.0, MASK_VALUE).astype(np.float32)
    in_specs = [
        pl.BlockSpec((N_HEADS, tile, HEAD_PAD), lambda b, i: (0, b * nq + i, 0)),
        pl.BlockSpec((N_HEADS, seq, HEAD_PAD), lambda b, i: (0, b, 0)),
        pl.BlockSpec((nq, ATT_WIDTH, tile), lambda b, i: (b, 0, 0)),
        _const_spec((diag, diag)),
    ]
    args = [q, k, vt, tri]
    if has_meta:
        km, vtm = meta_kv
        in_specs += [_const_spec(km.shape), _const_spec(vtm.shape)]
        args += [km, vtm]
    body = functools.partial(_attention_body, tile=tile, diag=diag, has_meta=has_meta,
                             group=HEAD_GROUP)
    return pl.pallas_call(
        body,
        grid=(batch, nq),
        in_specs=in_specs,
        out_specs=pl.BlockSpec((tile, ATT_WIDTH), lambda b, i: (b * nq + i, 0)),
        out_shape=jax.ShapeDtypeStruct((rows, ATT_WIDTH), jnp.bfloat16),
        scratch_shapes=[
            pltpu.VMEM((2, HEAD_GROUP, tile + N_META, tile), jnp.float32),
            pltpu.VMEM((2, HEAD_GROUP, 1, tile), jnp.float32),
            pltpu.VMEM((N_HEADS, 1, tile), jnp.float32),
            pltpu.VMEM((N_HEADS, V_DIM + ONES_ROWS, tile), jnp.float32),
        ],
        compiler_params=pltpu.CompilerParams(
            dimension_semantics=("parallel", "parallel"), vmem_limit_bytes=VMEM_LIMIT),
        name="attention",
    )(*args)


def _post_attention_body(h_ref, a_ref, b_ref, gmix_ref, wgates_ref, wpa_ref, wpb_ref, wo_ref,
                         gffn_ref, wgate_ref, wup_ref, wdown_ref, gfin_ref, o_ref, *, final_norm):
    h = h_ref[...]
    hn = _rmsnorm(h, gmix_ref[...]).astype(jnp.bfloat16)
    gates = jax.nn.sigmoid(_dot(hn, wgates_ref[...]))
    merged = (gates[:, :D_MODEL] * _dot(a_ref[...], wpa_ref[...])
              + gates[:, D_MODEL:] * _dot(b_ref[...], wpb_ref[...]))
    h = h + _dot(merged.astype(jnp.bfloat16), wo_ref[...])
    hn = _rmsnorm(h, gffn_ref[...]).astype(jnp.bfloat16)
    act = jax.nn.silu(_dot(hn, wgate_ref[...])) * _dot(hn, wup_ref[...])
    h = h + _dot(act.astype(jnp.bfloat16), wdown_ref[...])
    if final_norm:
        h = _rmsnorm(h, gfin_ref[...])
    o_ref[...] = h


def _post_attention(h, a, b, lw, layer, g_final, *, tile, final_norm):
    rows = h.shape[0]
    row_map = lambda r: (r, 0)
    body = functools.partial(_post_attention_body, final_norm=final_norm)
    return pl.pallas_call(
        body,
        grid=(rows // tile,),
        in_specs=[
            pl.BlockSpec((tile, D_MODEL), row_map),
            pl.BlockSpec((tile, POOL_WIDTH), row_map),
            pl.BlockSpec((tile, ATT_WIDTH), row_map),
            _const_spec((1, D_MODEL), layer),
            _const_spec((D_MODEL, 2 * D_MODEL), layer),
            _const_spec((POOL_WIDTH, D_MODEL), layer),
            _const_spec((ATT_WIDTH, D_MODEL), layer),
            _const_spec((D_MODEL, D_MODEL), layer),
            _const_spec((1, D_MODEL), layer),
            _const_spec((D_MODEL, D_FF), layer),
            _const_spec((D_MODEL, D_FF), layer),
            _const_spec((D_FF, D_MODEL), layer),
            _const_spec((1, D_MODEL)),
        ],
        out_specs=pl.BlockSpec((tile, D_MODEL), row_map),
        out_shape=jax.ShapeDtypeStruct((rows, D_MODEL), jnp.float32),
        compiler_params=pltpu.CompilerParams(
            dimension_semantics=("parallel",), vmem_limit_bytes=VMEM_LIMIT),
        name="post_attention",
    )(h, a, b, lw["g_mix"], lw["w_in_gates"], lw["w_pa"], lw["w_pb"], lw["w_o"], lw["g_ffn"],
      lw["w_gate"], lw["w_up"], lw["w_down"], g_final)


def _head_lanes():
    half = QK_ROPE // 2
    split = LANES // 2 - half
    return [(QK_NOPE, QK_NOPE + half), (0, split), (QK_NOPE + half, QK_DIM), (split, QK_NOPE),
            (None, HEAD_PAD - QK_DIM)]


def _place_head_columns(w, keep, xp=jnp):
    parts = []
    for start, stop in _head_lanes():
        if start is None or not keep(start):
            width = stop if start is None else stop - start
            parts.append(xp.zeros(w.shape[:-1] + (width,), w.dtype))
        else:
            parts.append(w[..., start:stop])
    return xp.concatenate(parts, axis=-1)


def _prepare_weights(norm_mix_g, w_in, pool_w, pool_scale, q_norm_g, kv_norm_g, w_uq, w_ukv,
                     w_pa, w_pb, w_o, norm_ffn_g, w_gate, w_up, w_down):
    bf = jnp.bfloat16
    depth = w_in.shape[0]
    o_q = POOL_WIDTH + Q_RANK
    o_kv = o_q + KV_RANK
    o_kr = o_kv + QK_ROPE
    is_nope = lambda col: col < QK_NOPE
    kr_cols = _place_head_columns(jnp.pad(w_in[..., o_kv:o_kr], ((0, 0), (0, 0), (QK_NOPE, 0))),
                                  lambda col: not is_nope(col))
    w_in_a = jnp.concatenate([w_in[..., :o_kv], kr_cols], axis=-1).astype(bf)
    w_q = _place_head_columns(w_uq.reshape(depth, Q_RANK, N_HEADS, QK_DIM), lambda col: True)
    ukv = w_ukv.reshape(depth, KV_RANK, N_HEADS, QK_NOPE + V_DIM)
    w_k = _place_head_columns(jnp.pad(ukv[..., :QK_NOPE], [(0, 0)] * 3 + [(0, QK_ROPE)]), is_nope)
    w_vt = jnp.swapaxes(ukv[..., QK_NOPE:].reshape(depth, KV_RANK, ATT_WIDTH), 1, 2)
    row = lambda g: g[:, None, :]
    return dict(
        g_mix=row(norm_mix_g), w_in_a=w_in_a, w_in_gates=w_in[..., o_kr:].astype(bf),
        pool_w=pool_w.astype(bf), pool_scale=row(pool_scale),
        q_g=row(q_norm_g), kv_g=row(kv_norm_g),
        w_q=w_q.reshape(depth, Q_RANK, N_HEADS * HEAD_PAD).astype(bf),
        w_k=w_k.reshape(depth, KV_RANK, N_HEADS * HEAD_PAD).astype(bf), w_vt=w_vt.astype(bf),
        w_pa=w_pa.astype(bf), w_pb=w_pb.astype(bf), w_o=w_o.astype(bf),
        g_ffn=row(norm_ffn_g), w_gate=w_gate.astype(bf), w_up=w_up.astype(bf),
        w_down=w_down.astype(bf))


def _rope_tables(length):
    f32 = np.float32
    inv = f32(1.0) / (f32(ROPE_THETA) ** (np.arange(0, QK_ROPE, 2, dtype=f32) / f32(QK_ROPE)))
    ang = np.arange(length, dtype=f32)[:, None] * inv[None, :]
    cos, sin = np.cos(ang).astype(f32), np.sin(ang).astype(f32)
    ones = np.ones((length, QK_NOPE), f32)
    cos_t = np.asarray(_place_head_columns(np.concatenate([ones, cos, cos], axis=1),
                                           lambda col: True, xp=np))
    sin_t = np.asarray(_place_head_columns(np.concatenate([0 * ones, -sin, sin], axis=1),
                                           lambda col: True, xp=np))
    return cos_t * f32(Q_SCALE), sin_t * f32(Q_SCALE), cos_t, sin_t


def kernel(x, meta_tokens, norm_mix_g, w_in, pool_w, pool_scale, q_norm_g, kv_norm_g, w_uq, w_ukv,
           w_pa, w_pb, w_o, norm_ffn_g, w_gate, w_up, w_down, final_norm_g):
    batch, seq, _ = x.shape
    depth = w_in.shape[0]
    tile = 512
    tables = _rope_tables(N_META + max(seq, META_TILE))
    meta_tables = tuple(t[:META_TILE] for t in tables)
    main_tables = tuple(t[N_META:N_META + seq] for t in tables)
    g_final = final_norm_g[None]
    lw = _prepare_weights(norm_mix_g, w_in, pool_w, pool_scale, q_norm_g, kv_norm_g, w_uq, w_ukv,
                          w_pa, w_pb, w_o, norm_ffn_g, w_gate, w_up, w_down)

    h = x.reshape(batch * seq, D_MODEL)
    hm = jnp.pad(meta_tokens.astype(x.dtype), ((0, META_TILE - N_META), (0, 0)))
    zero_halo = np.zeros((MAX_WINDOW, POOL_WIDTH), np.float32)
    for i in range(depth):
        last = i == depth - 1
        qm, km, vtm, am, um = _pre_attention(hm, lw, i, meta_tables, zero_halo, batch=1,
                                             seq=META_TILE, tile=META_TILE, pos_offset=0,
                                             emit_u=True)
        if not last:
            bm = _attention(qm, km, vtm, None, batch=1, seq=META_TILE, tile=META_TILE)
            hm = _post_attention(hm, am, bm, lw, i, g_final, tile=META_TILE, final_norm=False)
        q, k, vt, a = _pre_attention(h, lw, i, main_tables, um, batch=batch, seq=seq, tile=tile,
                                     pos_offset=N_META, emit_u=False)
        b = _attention(q, k, vt, (km, vtm), batch=batch, seq=seq, tile=tile)
        h = _post_attention(h, a, b, lw, i, g_final, tile=tile, final_norm=last)
    return h.reshape(batch, seq, D_MODEL)
```

```python
import functools
import math

import jax
import jax.numpy as jnp
import numpy as np
from jax import lax
from jax.experimental import pallas as pl
from jax.experimental.pallas import tpu as pltpu

D_MODEL = 1024
N_META = 16
POOL_WINDOWS = (2, 4, 8, 16)
POOL_GROUP = 128
POOL_WIDTH = POOL_GROUP * len(POOL_WINDOWS)
N_HEADS = 16
QK_NOPE = 64
QK_ROPE = 32
V_DIM = 64
Q_RANK = 256
KV_RANK = 128
QK_DIM = QK_NOPE + QK_ROPE
ATT_WIDTH = N_HEADS * V_DIM
ROPE_THETA = 10000.0
D_FF = 2816
NORM_EPS = 1e-6
MASK_VALUE = -1e30

LANES = 128
HEAD_PAD = LANES
MAX_WINDOW = max(POOL_WINDOWS)
POOL_PAD = 8
Q_SCALE = (QK_DIM ** -0.5) * math.log2(math.e)
META_TILE = 128
HEAD_GROUP = 2
ONES_ROWS = 16
POST_SPLIT = 2
MXU_ROWS = 256
SCORE_PITCH_PAD = 128
DIAG_BLOCK = 256
VMEM_LIMIT = 56 * 1024 * 1024

_NT = (((1,), (1,)), ((), ()))


def _rmsnorm(x, g):
    return x * lax.rsqrt(jnp.mean(x * x, axis=-1, keepdims=True) + NORM_EPS) * g


def _dot(a, b):
    return jnp.dot(a, b, preferred_element_type=jnp.float32)


def _const_spec(shape, layer=None):
    zeros = (0,) * len(shape)
    if layer is None:
        return pl.BlockSpec(shape, lambda *_: zeros, pipeline_mode=pl.Buffered(1))
    return pl.BlockSpec((None,) + tuple(shape), lambda *_: (layer,) + zeros,
                        pipeline_mode=pl.Buffered(1))


def _pre_attention_body(h_ref, gmix_ref, win_ref, poolw_ref, pscale_ref, qg_ref, kvg_ref,
                        wq_ref, wk_ref, wvt_ref, cosq_ref, sinq_ref, cosk_ref, sink_ref, halo_ref,
                        q_out, k_out, vt_out, a_out, *rest, tile, pos_offset, emit_u):
    if emit_u:
        u_out, pool_scr = rest
    else:
        (pool_scr,) = rest
    j = pl.program_id(1)
    row0 = pl.multiple_of(j * tile, tile)
    rows = pl.ds(row0, tile)
    first, end = POOL_PAD + MAX_WINDOW, POOL_PAD + MAX_WINDOW + tile

    @pl.when(j == 0)
    def _():
        pool_scr[:, 0:POOL_PAD, :] = jnp.zeros((3, POOL_PAD, POOL_WIDTH), jnp.float32)
        pool_scr[0, POOL_PAD:first, :] = halo_ref[...]

    hn = _rmsnorm(h_ref[...], gmix_ref[...]).astype(jnp.bfloat16)
    z = _dot(hn, win_ref[:, POOL_WIDTH:])
    u = _dot(hn, win_ref[:, 0:POOL_WIDTH])

    c_q = z[:, 0:Q_RANK]
    c_kv = z[:, Q_RANK:Q_RANK + KV_RANK]
    kr = z[:, Q_RANK + KV_RANK:]

    def rope(x, cos, sin):
        return x * cos + pltpu.roll(x, LANES // 2, 1) * sin

    cqn = _rmsnorm(c_q, qg_ref[...]).astype(jnp.bfloat16)
    q_raw = _dot(cqn, wq_ref[...])
    cos_q, sin_q = cosq_ref[rows, :], sinq_ref[rows, :]
    for hd in range(N_HEADS):
        x = q_raw[:, hd * HEAD_PAD:(hd + 1) * HEAD_PAD]
        q_out[hd] = rope(x, cos_q, sin_q).astype(jnp.bfloat16)

    ckvn = _rmsnorm(c_kv, kvg_ref[...]).astype(jnp.bfloat16)
    k_raw = _dot(ckvn, wk_ref[...])
    kr_roped = rope(kr, cosk_ref[rows, :], sink_ref[rows, :])
    for hd in range(N_HEADS):
        k_out[hd] = (k_raw[:, hd * HEAD_PAD:(hd + 1) * HEAD_PAD] + kr_roped).astype(jnp.bfloat16)
    vt = lax.dot_general(wvt_ref[...], ckvn, _NT, preferred_element_type=jnp.float32)
    vt_out[0] = vt.astype(jnp.bfloat16)

    pool_scr[0, first:end, :] = u
    if emit_u:
        u_out[...] = u
    g1, g2, g3 = POOL_GROUP, 2 * POOL_GROUP, 3 * POOL_GROUP
    pool_scr[1, POOL_PAD:end, :] = pool_scr[0, POOL_PAD:end, :] + pool_scr[0, POOL_PAD - 1:end - 1, :]
    pool_scr[2, POOL_PAD:end, g1:] = (pool_scr[1, POOL_PAD:end, g1:]
                                      + pool_scr[1, POOL_PAD - 2:end - 2, g1:])
    pool_scr[1, POOL_PAD:end, g2:] = (pool_scr[2, POOL_PAD:end, g2:]
                                      + pool_scr[2, POOL_PAD - 4:end - 4, g2:])
    wsums = [pool_scr[1, first:end, 0:g1], pool_scr[2, first:end, g1:g2],
             pool_scr[1, first:end, g2:g3],
             pool_scr[1, first:end, g3:] + pool_scr[1, first - 8:end - 8, g3:]]
    pos = pos_offset + row0 + lax.broadcasted_iota(jnp.int32, (tile, POOL_GROUP), 0)
    outs = []
    for g, w in enumerate(POOL_WINDOWS):
        count = jnp.minimum(pos + 1, w).astype(jnp.float32)
        y = (wsums[g] / count - u[:, g * POOL_GROUP:(g + 1) * POOL_GROUP]).astype(jnp.bfloat16)
        outs.append(_dot(y, poolw_ref[g]))
    a = jnp.concatenate(outs, axis=1) * pscale_ref[...]
    a_out[...] = a.astype(jnp.bfloat16)
    pool_scr[0, POOL_PAD:first, :] = pool_scr[0, end - MAX_WINDOW:end, :]


def _pre_attention(h, lw, layer, tables, halo, *, batch, seq, tile, pos_offset, emit_u):
    rows = batch * seq
    n_tiles = seq // tile
    row_map = lambda b, j: (b * n_tiles + j, 0)
    in_specs = [
        pl.BlockSpec((tile, D_MODEL), row_map),
        _const_spec((1, D_MODEL), layer),
        _const_spec((D_MODEL, 1024), layer),
        _const_spec((len(POOL_WINDOWS), POOL_GROUP, POOL_GROUP), layer),
        _const_spec((1, POOL_WIDTH), layer),
        _const_spec((1, Q_RANK), layer),
        _const_spec((1, KV_RANK), layer),
        _const_spec((Q_RANK, N_HEADS * HEAD_PAD), layer),
        _const_spec((KV_RANK, N_HEADS * HEAD_PAD), layer),
        _const_spec((ATT_WIDTH, KV_RANK), layer),
        _const_spec((seq, LANES)),
        _const_spec((seq, LANES)),
        _const_spec((seq, LANES)),
        _const_spec((seq, LANES)),
        pl.BlockSpec((MAX_WINDOW, POOL_WIDTH), lambda b, j: (0, 0)),
    ]
    out_shape = [
        jax.ShapeDtypeStruct((N_HEADS, rows, HEAD_PAD), jnp.bfloat16),
        jax.ShapeDtypeStruct((N_HEADS, rows, HEAD_PAD), jnp.bfloat16),
        jax.ShapeDtypeStruct((rows // tile, ATT_WIDTH, tile), jnp.bfloat16),
        jax.ShapeDtypeStruct((rows, POOL_WIDTH), jnp.bfloat16),
    ]
    out_specs = [
        pl.BlockSpec((N_HEADS, tile, HEAD_PAD), lambda b, j: (0, b * n_tiles + j, 0)),
        pl.BlockSpec((N_HEADS, tile, HEAD_PAD), lambda b, j: (0, b * n_tiles + j, 0)),
        pl.BlockSpec((1, ATT_WIDTH, tile), lambda b, j: (b * n_tiles + j, 0, 0)),
        pl.BlockSpec((tile, POOL_WIDTH), row_map),
    ]
    if emit_u:
        out_shape.append(jax.ShapeDtypeStruct((rows, POOL_WIDTH), jnp.float32))
        out_specs.append(pl.BlockSpec((tile, POOL_WIDTH), row_map))
    body = functools.partial(_pre_attention_body, tile=tile, pos_offset=pos_offset, emit_u=emit_u)
    return pl.pallas_call(
        body,
        grid=(batch, n_tiles),
        in_specs=in_specs,
        out_specs=out_specs,
        out_shape=out_shape,
        scratch_shapes=[pltpu.VMEM((3, POOL_PAD + MAX_WINDOW + tile, POOL_WIDTH), jnp.float32)],
        compiler_params=pltpu.CompilerParams(
            dimension_semantics=("arbitrary", "arbitrary"), vmem_limit_bytes=VMEM_LIMIT),
        name="pre_attention",
    )(h, lw["g_mix"], lw["w_in_a"], lw["pool_w"], lw["pool_scale"], lw["q_g"], lw["kv_g"],
      lw["w_q"], lw["w_k"], lw["w_vt"], *tables, halo)


def _attention_body(*refs, tile, diag, has_meta, group):
    if has_meta:
        q_ref, k_ref, vt_ref, tri_ref, km_ref, vtm_ref, o_ref, s_scr, smax_scr, m_scr, acc_scr = refs
    else:
        q_ref, k_ref, vt_ref, tri_ref, o_ref, s_scr, smax_scr, m_scr, acc_scr = refs
    i = pl.program_id(1)
    n_groups = N_HEADS // group
    n_sub = tile // diag

    def produce_full(slot, hg, t):
        k_rows = pl.ds(pl.multiple_of(t * tile, tile), tile)
        for g in range(group):
            hd = hg * group + g
            s = lax.dot_general(k_ref[hd, k_rows, :], q_ref[hd], _NT,
                                preferred_element_type=jnp.float32)
            s_scr[slot, g, 0:tile, 0:tile] = s
            smax_scr[slot, g] = jnp.max(s, axis=0, keepdims=True)

    def produce_diag(slot, hg, t):
        for g in range(group):
            hd = hg * group + g
            q = q_ref[hd]
            smax = None
            for c in range(n_sub):
                lo = c * diag
                k = k_ref[hd, pl.ds(pl.multiple_of(t * tile + lo, diag), diag), :]
                if c == 0 and has_meta:
                    k = jnp.concatenate([k, km_ref[hd, 0:N_META, :]], axis=0)
                s = lax.dot_general(k, q[lo:, :], _NT, preferred_element_type=jnp.float32)
                on_diag = s[0:diag, 0:diag] + tri_ref[...]
                main = on_diag if lo + diag == tile else jnp.concatenate(
                    [on_diag, s[0:diag, diag:]], axis=1)
                s_scr[slot, g, lo:lo + diag, lo:tile] = main
                cmax = jnp.max(main, axis=0, keepdims=True)
                if c == 0 and has_meta:
                    s_scr[slot, g, tile:tile + N_META, 0:tile] = s[diag:, :]
                    cmax = jnp.maximum(cmax, jnp.max(s[diag:, :], axis=0, keepdims=True))
                smax = cmax if c == 0 else jnp.concatenate(
                    [smax[:, :lo], jnp.maximum(smax[:, lo:], cmax)], axis=1)
            smax_scr[slot, g] = smax

    def consume(slot, hg, t, is_diag):
        for g in range(group):
            hd = hg * group + g
            v_rows = pl.ds(pl.multiple_of(hd * V_DIM, V_DIM), V_DIM)
            m_old = m_scr[hd]
            m_new = jnp.maximum(m_old, smax_scr[slot, g])
            if not is_diag:
                p = jnp.exp2(s_scr[slot, g, 0:tile, 0:tile] - m_new).astype(jnp.bfloat16)
                ones = jnp.ones((ONES_ROWS, tile), jnp.bfloat16)
                pv = _dot(jnp.concatenate([vt_ref[t, v_rows, :], ones], axis=0), p)
            else:
                ones = jnp.ones((ONES_ROWS, diag), jnp.bfloat16)
                for c in range(n_sub):
                    lo = c * diag
                    p = jnp.exp2(s_scr[slot, g, lo:lo + diag, lo:tile] - m_new[:, lo:]).astype(jnp.bfloat16)
                    part = _dot(jnp.concatenate([vt_ref[t, v_rows, lo:lo + diag], ones], axis=0), p)
                    pv = part if c == 0 else jnp.concatenate(
                        [pv[:, :lo], pv[:, lo:] + part], axis=1)
                if has_meta:
                    pm = jnp.exp2(s_scr[slot, g, tile:tile + N_META, 0:tile] - m_new).astype(jnp.bfloat16)
                    ones_meta = jnp.ones((ONES_ROWS, N_META), jnp.bfloat16)
                    pv = pv + _dot(
                        jnp.concatenate([vtm_ref[0, v_rows, 0:N_META], ones_meta], axis=0), pm)
            acc_scr[hd] = jnp.exp2(m_old - m_new) * acc_scr[hd] + pv
            m_scr[hd] = m_new

    def finalize_pair(pair):
        acc0, acc1 = acc_scr[2 * pair], acc_scr[2 * pair + 1]
        both = jnp.concatenate([acc0[0:V_DIM] / acc0[V_DIM:V_DIM + 1],
                                acc1[0:V_DIM] / acc1[V_DIM:V_DIM + 1]], axis=0)
        o_ref[:, pair * LANES:(pair + 1) * LANES] = both.T.astype(jnp.bfloat16)

    def key_block(t, is_diag, next_diag):
        for hg in range(n_groups):
            if hg + 1 < n_groups:
                (produce_diag if is_diag else produce_full)((hg + 1) % 2, hg + 1, t)
            elif next_diag is not None:
                (produce_diag if next_diag else produce_full)(0, 0, t + 1)
            consume(hg % 2, hg, t, is_diag)
            if is_diag:
                for hd in range(hg * group, (hg + 1) * group, 2):
                    finalize_pair(hd // 2)

    m_scr[...] = jnp.full_like(m_scr, MASK_VALUE)
    acc_scr[...] = jnp.zeros_like(acc_scr)

    @pl.when(i > 0)
    def _():
        produce_full(0, 0, 0)

    def full_block(t, carry):
        key_block(t, False, False)
        return carry
    lax.fori_loop(0, i - 1, full_block, 0)

    @pl.when(i > 0)
    def _():
        key_block(i - 1, False, True)

    @pl.when(i == 0)
    def _():
        produce_diag(0, 0, i)

    key_block(i, True, None)


def _attention(q, k, vt, meta_kv, *, batch, seq, tile):
    rows = batch * seq
    nq = seq // tile
    has_meta = meta_kv is not None
    assert (N_HEADS // HEAD_GROUP) % 2 == 0 and HEAD_GROUP % 2 == 0
    diag = min(tile, DIAG_BLOCK)
    idx = np.arange(diag)
    tri = np.where(idx[:, None] <= idx[None, :], 0.0, MASK_VALUE).astype(np.float32)
    in_specs = [
        pl.BlockSpec((N_HEADS, tile, HEAD_PAD), lambda b, i: (0, b * nq + i, 0)),
        pl.BlockSpec((N_HEADS, seq, HEAD_PAD), lambda b, i: (0, b, 0)),
        pl.BlockSpec((nq, ATT_WIDTH, tile), lambda b, i: (b, 0, 0)),
        _const_spec((diag, diag)),
    ]
    args = [q, k, vt, tri]
    if has_meta:
        km, vtm = meta_kv
        in_specs += [_const_spec(km.shape), _const_spec(vtm.shape)]
        args += [km, vtm]
    body = functools.partial(_attention_body, tile=tile, diag=diag, has_meta=has_meta,
                             group=HEAD_GROUP)
    return pl.pallas_call(
        body,
        grid=(batch, nq),
        in_specs=in_specs,
        out_specs=pl.BlockSpec((tile, ATT_WIDTH), lambda b, i: (b * nq + i, 0)),
        out_shape=jax.ShapeDtypeStruct((rows, ATT_WIDTH), jnp.bfloat16),
        scratch_shapes=[
            pltpu.VMEM((2, HEAD_GROUP, tile + N_META, tile + SCORE_PITCH_PAD), jnp.float32),
            pltpu.VMEM((2, HEAD_GROUP, 1, tile), jnp.float32),
            pltpu.VMEM((N_HEADS, 1, tile), jnp.float32),
            pltpu.VMEM((N_HEADS, V_DIM + ONES_ROWS, tile), jnp.float32),
        ],
        compiler_params=pltpu.CompilerParams(
            dimension_semantics=("parallel", "parallel"), vmem_limit_bytes=VMEM_LIMIT),
        name="attention",
    )(*args)


def _post_attention_body(h_ref, a_ref, b_ref, gmix_ref, wgates_ref, wpa_ref, wpb_ref, wo_ref,
                         gffn_ref, wgate_ref, wup_ref, wdown_ref, gfin_ref, o_ref, *, final_norm):
    tile = h_ref.shape[0]
    step = min(tile, max(tile // POST_SPLIT, MXU_ROWS))
    parts = [pl.ds(r * step, step) for r in range(tile // step)]
    pa = [_dot(a_ref[r, :], wpa_ref[...]) for r in parts]
    pb = [_dot(b_ref[r, :], wpb_ref[...]) for r in parts]
    hs = [h_ref[r, :] for r in parts]
    hn = [_rmsnorm(h, gmix_ref[...]).astype(jnp.bfloat16) for h in hs]
    gates = [jax.nn.sigmoid(_dot(x, wgates_ref[...])) for x in hn]
    merged = [(g[:, :D_MODEL] * xa + g[:, D_MODEL:] * xb).astype(jnp.bfloat16)
              for g, xa, xb in zip(gates, pa, pb)]
    hs = [h + _dot(m, wo_ref[...]) for h, m in zip(hs, merged)]
    hn = [_rmsnorm(h, gffn_ref[...]).astype(jnp.bfloat16) for h in hs]
    act = [(jax.nn.silu(_dot(x, wgate_ref[...])) * _dot(x, wup_ref[...])).astype(jnp.bfloat16)
           for x in hn]
    hs = [h + _dot(x, wdown_ref[...]) for h, x in zip(hs, act)]
    for r, h in zip(parts, hs):
        o_ref[r, :] = _rmsnorm(h, gfin_ref[...]) if final_norm else h


def _post_attention(h, a, b, lw, layer, g_final, *, tile, final_norm):
    rows = h.shape[0]
    row_map = lambda r: (r, 0)
    body = functools.partial(_post_attention_body, final_norm=final_norm)
    return pl.pallas_call(
        body,
        grid=(rows // tile,),
        in_specs=[
            pl.BlockSpec((tile, D_MODEL), row_map),
            pl.BlockSpec((tile, POOL_WIDTH), row_map),
            pl.BlockSpec((tile, ATT_WIDTH), row_map),
            _const_spec((1, D_MODEL), layer),
            _const_spec((D_MODEL, 2 * D_MODEL), layer),
            _const_spec((POOL_WIDTH, D_MODEL), layer),
            _const_spec((ATT_WIDTH, D_MODEL), layer),
            _const_spec((D_MODEL, D_MODEL), layer),
            _const_spec((1, D_MODEL), layer),
            _const_spec((D_MODEL, D_FF), layer),
            _const_spec((D_MODEL, D_FF), layer),
            _const_spec((D_FF, D_MODEL), layer),
            _const_spec((1, D_MODEL)),
        ],
        out_specs=pl.BlockSpec((tile, D_MODEL), row_map),
        out_shape=jax.ShapeDtypeStruct((rows, D_MODEL), jnp.float32),
        compiler_params=pltpu.CompilerParams(
            dimension_semantics=("parallel",), vmem_limit_bytes=VMEM_LIMIT),
        name="post_attention",
    )(h, a, b, lw["g_mix"], lw["w_in_gates"], lw["w_pa"], lw["w_pb"], lw["w_o"], lw["g_ffn"],
      lw["w_gate"], lw["w_up"], lw["w_down"], g_final)


def _head_lanes():
    half = QK_ROPE // 2
    split = LANES // 2 - half
    return [(QK_NOPE, QK_NOPE + half), (0, split), (QK_NOPE + half, QK_DIM), (split, QK_NOPE),
            (None, HEAD_PAD - QK_DIM)]


def _place_head_columns(w, keep, xp=jnp):
    parts = []
    for start, stop in _head_lanes():
        if start is None or not keep(start):
            width = stop if start is None else stop - start
            parts.append(xp.zeros(w.shape[:-1] + (width,), w.dtype))
        else:
            parts.append(w[..., start:stop])
    return xp.concatenate(parts, axis=-1)


def _prepare_weights(norm_mix_g, w_in, pool_w, pool_scale, q_norm_g, kv_norm_g, w_uq, w_ukv,
                     w_pa, w_pb, w_o, norm_ffn_g, w_gate, w_up, w_down):
    bf = jnp.bfloat16
    depth = w_in.shape[0]
    o_q = POOL_WIDTH + Q_RANK
    o_kv = o_q + KV_RANK
    o_kr = o_kv + QK_ROPE
    is_nope = lambda col: col < QK_NOPE
    kr_cols = _place_head_columns(jnp.pad(w_in[..., o_kv:o_kr], ((0, 0), (0, 0), (QK_NOPE, 0))),
                                  lambda col: not is_nope(col))
    w_in_a = jnp.concatenate([w_in[..., :o_kv], kr_cols], axis=-1).astype(bf)
    w_q = _place_head_columns(w_uq.reshape(depth, Q_RANK, N_HEADS, QK_DIM), lambda col: True)
    ukv = w_ukv.reshape(depth, KV_RANK, N_HEADS, QK_NOPE + V_DIM)
    w_k = _place_head_columns(jnp.pad(ukv[..., :QK_NOPE], [(0, 0)] * 3 + [(0, QK_ROPE)]), is_nope)
    w_vt = jnp.swapaxes(ukv[..., QK_NOPE:].reshape(depth, KV_RANK, ATT_WIDTH), 1, 2)
    row = lambda g: g[:, None, :]
    return dict(
        g_mix=row(norm_mix_g), w_in_a=w_in_a, w_in_gates=w_in[..., o_kr:].astype(bf),
        pool_w=pool_w.astype(bf), pool_scale=row(pool_scale),
        q_g=row(q_norm_g), kv_g=row(kv_norm_g),
        w_q=w_q.reshape(depth, Q_RANK, N_HEADS * HEAD_PAD).astype(bf),
        w_k=w_k.reshape(depth, KV_RANK, N_HEADS * HEAD_PAD).astype(bf), w_vt=w_vt.astype(bf),
        w_pa=w_pa.astype(bf), w_pb=w_pb.astype(bf), w_o=w_o.astype(bf),
        g_ffn=row(norm_ffn_g), w_gate=w_gate.astype(bf), w_up=w_up.astype(bf),
        w_down=w_down.astype(bf))


def _rope_tables(length):
    f32 = np.float32
    inv = f32(1.0) / (f32(ROPE_THETA) ** (np.arange(0, QK_ROPE, 2, dtype=f32) / f32(QK_ROPE)))
    ang = np.arange(length, dtype=f32)[:, None] * inv[None, :]
    cos, sin = np.cos(ang).astype(f32), np.sin(ang).astype(f32)
    ones = np.ones((length, QK_NOPE), f32)
    cos_t = np.asarray(_place_head_columns(np.concatenate([ones, cos, cos], axis=1),
                                           lambda col: True, xp=np))
    sin_t = np.asarray(_place_head_columns(np.concatenate([0 * ones, -sin, sin], axis=1),
                                           lambda col: True, xp=np))
    return cos_t * f32(Q_SCALE), sin_t * f32(Q_SCALE), cos_t, sin_t


def kernel(x, meta_tokens, norm_mix_g, w_in, pool_w, pool_scale, q_norm_g, kv_norm_g, w_uq, w_ukv,
           w_pa, w_pb, w_o, norm_ffn_g, w_gate, w_up, w_down, final_norm_g):
    batch, seq, _ = x.shape
    depth = w_in.shape[0]
    tile = 512
    tables = _rope_tables(N_META + max(seq, META_TILE))
    meta_tables = tuple(t[:META_TILE] for t in tables)
    main_tables = tuple(t[N_META:N_META + seq] for t in tables)
    g_final = final_norm_g[None]
    lw = _prepare_weights(norm_mix_g, w_in, pool_w, pool_scale, q_norm_g, kv_norm_g, w_uq, w_ukv,
                          w_pa, w_pb, w_o, norm_ffn_g, w_gate, w_up, w_down)

    h = x.reshape(batch * seq, D_MODEL)
    hm = jnp.pad(meta_tokens.astype(x.dtype), ((0, META_TILE - N_META), (0, 0)))
    zero_halo = np.zeros((MAX_WINDOW, POOL_WIDTH), np.float32)
    for i in range(depth):
        last = i == depth - 1
        qm, km, vtm, am, um = _pre_attention(hm, lw, i, meta_tables, zero_halo, batch=1,
                                             seq=META_TILE, tile=META_TILE, pos_offset=0,
                                             emit_u=True)
        if not last:
            bm = _attention(qm, km, vtm, None, batch=1, seq=META_TILE, tile=META_TILE)
            hm = _post_attention(hm, am, bm, lw, i, g_final, tile=META_TILE, final_norm=False)
        q, k, vt, a = _pre_attention(h, lw, i, main_tables, um, batch=batch, seq=seq, tile=tile,
                                     pos_offset=N_META, emit_u=False)
        b = _attention(q, k, vt, (km, vtm), batch=batch, seq=seq, tile=tile)
        h = _post_attention(h, a, b, lw, i, g_final, tile=tile, final_norm=last)
    return h.reshape(batch, seq, D_MODEL)
```

```python
import functools
import math

import jax
import jax.numpy as jnp
import numpy as np
from jax import lax
from jax.experimental import pallas as pl
from jax.experimental.pallas import tpu as pltpu

D_MODEL = 1024
N_META = 16
POOL_WINDOWS = (2, 4, 8, 16)
POOL_GROUP = 128
POOL_WIDTH = POOL_GROUP * len(POOL_WINDOWS)
N_HEADS = 16
QK_NOPE = 64
QK_ROPE = 32
V_DIM = 64
Q_RANK = 256
KV_RANK = 128
QK_DIM = QK_NOPE + QK_ROPE
ROPE_HALF = QK_ROPE // 2
ATT_WIDTH = N_HEADS * V_DIM
ROPE_THETA = 10000.0
D_FF = 2816
NORM_EPS = 1e-6
MASK_VALUE = -1e30

LANES = 128
HEAD_PAD = LANES
MAX_WINDOW = max(POOL_WINDOWS)
POOL_PAD = 8
Q_SCALE = (QK_DIM ** -0.5) * math.log2(math.e)
META_TILE = 128
HEAD_GROUP = 2
ONES_ROWS = 16
POST_SPLIT = 2
MXU_ROWS = 256
SCORE_PITCH_PAD = 128
DIAG_BLOCK = 256
VMEM_LIMIT = 56 * 1024 * 1024

_NT = (((1,), (1,)), ((), ()))


def _rmsnorm(x, g):
    return x * lax.rsqrt(jnp.mean(x * x, axis=-1, keepdims=True) + NORM_EPS) * g


def _dot(a, b):
    return jnp.dot(a, b, preferred_element_type=jnp.float32)


def _const_spec(shape, layer=None):
    zeros = (0,) * len(shape)
    if layer is None:
        return pl.BlockSpec(shape, lambda *_: zeros, pipeline_mode=pl.Buffered(1))
    return pl.BlockSpec((None,) + tuple(shape), lambda *_: (layer,) + zeros,
                        pipeline_mode=pl.Buffered(1))


def _pre_attention_body(h_ref, gmix_ref, win_ref, poolw_ref, pscale_ref, qg_ref, kvg_ref,
                        wqt_ref, wk_ref, wvt_ref, cosq_ref, sinq_ref, cosk_ref, sink_ref, halo_ref,
                        q_out, k_out, vt_out, a_out, *rest, tile, pos_offset, emit_u):
    if emit_u:
        u_out, pool_scr = rest
    else:
        (pool_scr,) = rest
    j = pl.program_id(1)
    row0 = pl.multiple_of(j * tile, tile)
    rows = pl.ds(row0, tile)
    first, end = POOL_PAD + MAX_WINDOW, POOL_PAD + MAX_WINDOW + tile

    @pl.when(j == 0)
    def _():
        pool_scr[:, 0:POOL_PAD, :] = jnp.zeros((3, POOL_PAD, POOL_WIDTH), jnp.float32)
        pool_scr[0, POOL_PAD:first, :] = halo_ref[...]

    hn = _rmsnorm(h_ref[...], gmix_ref[...]).astype(jnp.bfloat16)
    z = _dot(hn, win_ref[:, POOL_WIDTH:])
    u = _dot(hn, win_ref[:, 0:POOL_WIDTH])

    c_q = z[:, 0:Q_RANK]
    c_kv = z[:, Q_RANK:Q_RANK + KV_RANK]
    kr = z[:, Q_RANK + KV_RANK:]

    cqn = _rmsnorm(c_q, qg_ref[...]).astype(jnp.bfloat16)
    qt_raw = lax.dot_general(wqt_ref[...], cqn, _NT, preferred_element_type=jnp.float32)
    cos_q, sin_q = cosq_ref[j], sinq_ref[j]
    for hd in range(N_HEADS):
        x = qt_raw[hd * HEAD_PAD:(hd + 1) * HEAD_PAD, :]
        partner = jnp.concatenate([x[ROPE_HALF:QK_ROPE], x[:ROPE_HALF], x[QK_ROPE:]], axis=0)
        q_out[hd * HEAD_PAD:(hd + 1) * HEAD_PAD, :] = (x * cos_q + partner * sin_q).astype(jnp.bfloat16)

    ckvn = _rmsnorm(c_kv, kvg_ref[...]).astype(jnp.bfloat16)
    k_raw = _dot(ckvn, wk_ref[...])
    lane = lax.broadcasted_iota(jnp.int32, kr.shape, 1)
    kr_partner = jnp.where(lane < ROPE_HALF, pltpu.roll(kr, LANES - ROPE_HALF, 1),
                           pltpu.roll(kr, ROPE_HALF, 1))
    kr_roped = kr * cosk_ref[rows, :] + kr_partner * sink_ref[rows, :]
    for hd in range(N_HEADS):
        k_out[hd] = (k_raw[:, hd * HEAD_PAD:(hd + 1) * HEAD_PAD] + kr_roped).astype(jnp.bfloat16)
    vt = lax.dot_general(wvt_ref[...], ckvn, _NT, preferred_element_type=jnp.float32)
    vt_out[0] = vt.astype(jnp.bfloat16)

    pool_scr[0, first:end, :] = u
    if emit_u:
        u_out[...] = u
    g1, g2, g3 = POOL_GROUP, 2 * POOL_GROUP, 3 * POOL_GROUP
    pool_scr[1, POOL_PAD:end, :] = pool_scr[0, POOL_PAD:end, :] + pool_scr[0, POOL_PAD - 1:end - 1, :]
    pool_scr[2, POOL_PAD:end, g1:] = (pool_scr[1, POOL_PAD:end, g1:]
                                      + pool_scr[1, POOL_PAD - 2:end - 2, g1:])
    pool_scr[1, POOL_PAD:end, g2:] = (pool_scr[2, POOL_PAD:end, g2:]
                                      + pool_scr[2, POOL_PAD - 4:end - 4, g2:])
    wsums = [pool_scr[1, first:end, 0:g1], pool_scr[2, first:end, g1:g2],
             pool_scr[1, first:end, g2:g3],
             pool_scr[1, first:end, g3:] + pool_scr[1, first - 8:end - 8, g3:]]
    pos = pos_offset + row0 + lax.broadcasted_iota(jnp.int32, (tile, POOL_GROUP), 0)
    outs = []
    for g, w in enumerate(POOL_WINDOWS):
        count = jnp.minimum(pos + 1, w).astype(jnp.float32)
        y = (wsums[g] / count - u[:, g * POOL_GROUP:(g + 1) * POOL_GROUP]).astype(jnp.bfloat16)
        outs.append(_dot(y, poolw_ref[g]))
    a = jnp.concatenate(outs, axis=1) * pscale_ref[...]
    a_out[...] = a.astype(jnp.bfloat16)
    pool_scr[0, POOL_PAD:first, :] = pool_scr[0, end - MAX_WINDOW:end, :]


def _pre_attention(h, lw, layer, tables, halo, *, batch, seq, tile, pos_offset, emit_u):
    rows = batch * seq
    n_tiles = seq // tile
    row_map = lambda b, j: (b * n_tiles + j, 0)
    in_specs = [
        pl.BlockSpec((tile, D_MODEL), row_map),
        _const_spec((1, D_MODEL), layer),
        _const_spec((D_MODEL, 1024), layer),
        _const_spec((len(POOL_WINDOWS), POOL_GROUP, POOL_GROUP), layer),
        _const_spec((1, POOL_WIDTH), layer),
        _const_spec((1, Q_RANK), layer),
        _const_spec((1, KV_RANK), layer),
        _const_spec((N_HEADS * HEAD_PAD, Q_RANK), layer),
        _const_spec((KV_RANK, N_HEADS * HEAD_PAD), layer),
        _const_spec((ATT_WIDTH, KV_RANK), layer),
        _const_spec((n_tiles, HEAD_PAD, tile)),
        _const_spec((n_tiles, HEAD_PAD, tile)),
        _const_spec((seq, LANES)),
        _const_spec((seq, LANES)),
        pl.BlockSpec((MAX_WINDOW, POOL_WIDTH), lambda b, j: (0, 0)),
    ]
    out_shape = [
        jax.ShapeDtypeStruct((N_HEADS * HEAD_PAD, rows), jnp.bfloat16),
        jax.ShapeDtypeStruct((N_HEADS, rows, HEAD_PAD), jnp.bfloat16),
        jax.ShapeDtypeStruct((rows // tile, ATT_WIDTH, tile), jnp.bfloat16),
        jax.ShapeDtypeStruct((rows, POOL_WIDTH), jnp.bfloat16),
    ]
    out_specs = [
        pl.BlockSpec((N_HEADS * HEAD_PAD, tile), lambda b, j: (0, b * n_tiles + j)),
        pl.BlockSpec((N_HEADS, tile, HEAD_PAD), lambda b, j: (0, b * n_tiles + j, 0)),
        pl.BlockSpec((1, ATT_WIDTH, tile), lambda b, j: (b * n_tiles + j, 0, 0)),
        pl.BlockSpec((tile, POOL_WIDTH), row_map),
    ]
    if emit_u:
        out_shape.append(jax.ShapeDtypeStruct((rows, POOL_WIDTH), jnp.float32))
        out_specs.append(pl.BlockSpec((tile, POOL_WIDTH), row_map))
    body = functools.partial(_pre_attention_body, tile=tile, pos_offset=pos_offset, emit_u=emit_u)
    return pl.pallas_call(
        body,
        grid=(batch, n_tiles),
        in_specs=in_specs,
        out_specs=out_specs,
        out_shape=out_shape,
        scratch_shapes=[pltpu.VMEM((3, POOL_PAD + MAX_WINDOW + tile, POOL_WIDTH), jnp.float32)],
        compiler_params=pltpu.CompilerParams(
            dimension_semantics=("arbitrary", "arbitrary"), vmem_limit_bytes=VMEM_LIMIT),
        name="pre_attention",
    )(h, lw["g_mix"], lw["w_in_a"], lw["pool_w"], lw["pool_scale"], lw["q_g"], lw["kv_g"],
      lw["w_qt"], lw["w_k"], lw["w_vt"], *tables, halo)


def _attention_body(*refs, tile, diag, has_meta, group):
    if has_meta:
        q_ref, k_ref, vt_ref, tri_ref, km_ref, vtm_ref, o_ref, s_scr, smax_scr, m_scr, acc_scr = refs
    else:
        q_ref, k_ref, vt_ref, tri_ref, o_ref, s_scr, smax_scr, m_scr, acc_scr = refs
    i = pl.program_id(1)
    n_groups = N_HEADS // group
    n_sub = tile // diag

    def produce_full(slot, hg, t):
        k_rows = pl.ds(pl.multiple_of(t * tile, tile), tile)
        for g in range(group):
            hd = hg * group + g
            s = _dot(k_ref[hd, k_rows, :], q_ref[hd * HEAD_PAD:(hd + 1) * HEAD_PAD, :])
            s_scr[slot, g, 0:tile, 0:tile] = s
            smax_scr[slot, g] = jnp.max(s, axis=0, keepdims=True)

    def produce_diag(slot, hg, t):
        for g in range(group):
            hd = hg * group + g
            q = q_ref[hd * HEAD_PAD:(hd + 1) * HEAD_PAD, :]
            smax = None
            for c in range(n_sub):
                lo = c * diag
                k = k_ref[hd, pl.ds(pl.multiple_of(t * tile + lo, diag), diag), :]
                if c == 0 and has_meta:
                    k = jnp.concatenate([k, km_ref[hd, 0:N_META, :]], axis=0)
                s = _dot(k, q[:, lo:])
                on_diag = s[0:diag, 0:diag] + tri_ref[...]
                main = on_diag if lo + diag == tile else jnp.concatenate(
                    [on_diag, s[0:diag, diag:]], axis=1)
                s_scr[slot, g, lo:lo + diag, lo:tile] = main
                cmax = jnp.max(main, axis=0, keepdims=True)
                if c == 0 and has_meta:
                    s_scr[slot, g, tile:tile + N_META, 0:tile] = s[diag:, :]
                    cmax = jnp.maximum(cmax, jnp.max(s[diag:, :], axis=0, keepdims=True))
                smax = cmax if c == 0 else jnp.concatenate(
                    [smax[:, :lo], jnp.maximum(smax[:, lo:], cmax)], axis=1)
            smax_scr[slot, g] = smax

    def consume(slot, hg, t, is_diag):
        for g in range(group):
            hd = hg * group + g
            v_rows = pl.ds(pl.multiple_of(hd * V_DIM, V_DIM), V_DIM)
            m_old = m_scr[hd]
            m_new = jnp.maximum(m_old, smax_scr[slot, g])
            if not is_diag:
                p = jnp.exp2(s_scr[slot, g, 0:tile, 0:tile] - m_new).astype(jnp.bfloat16)
                ones = jnp.ones((ONES_ROWS, tile), jnp.bfloat16)
                pv = _dot(jnp.concatenate([vt_ref[t, v_rows, :], ones], axis=0), p)
            else:
                ones = jnp.ones((ONES_ROWS, diag), jnp.bfloat16)
                for c in range(n_sub):
                    lo = c * diag
                    p = jnp.exp2(s_scr[slot, g, lo:lo + diag, lo:tile] - m_new[:, lo:]).astype(jnp.bfloat16)
                    part = _dot(jnp.concatenate([vt_ref[t, v_rows, lo:lo + diag], ones], axis=0), p)
                    pv = part if c == 0 else jnp.concatenate(
                        [pv[:, :lo], pv[:, lo:] + part], axis=1)
                if has_meta:
                    pm = jnp.exp2(s_scr[slot, g, tile:tile + N_META, 0:tile] - m_new).astype(jnp.bfloat16)
                    ones_meta = jnp.ones((ONES_ROWS, N_META), jnp.bfloat16)
                    pv = pv + _dot(
                        jnp.concatenate([vtm_ref[0, v_rows, 0:N_META], ones_meta], axis=0), pm)
            acc_scr[hd] = jnp.exp2(m_old - m_new) * acc_scr[hd] + pv
            m_scr[hd] = m_new

    def finalize_pair(pair):
        acc0, acc1 = acc_scr[2 * pair], acc_scr[2 * pair + 1]
        both = jnp.concatenate([acc0[0:V_DIM] / acc0[V_DIM:V_DIM + 1],
                                acc1[0:V_DIM] / acc1[V_DIM:V_DIM + 1]], axis=0)
        o_ref[:, pair * LANES:(pair + 1) * LANES] = both.T.astype(jnp.bfloat16)

    def key_block(t, is_diag, next_diag):
        for hg in range(n_groups):
            if hg + 1 < n_groups:
                (produce_diag if is_diag else produce_full)((hg + 1) % 2, hg + 1, t)
            elif next_diag is not None:
                (produce_diag if next_diag else produce_full)(0, 0, t + 1)
            consume(hg % 2, hg, t, is_diag)
            if is_diag:
                for hd in range(hg * group, (hg + 1) * group, 2):
                    finalize_pair(hd // 2)

    m_scr[...] = jnp.full_like(m_scr, MASK_VALUE)
    acc_scr[...] = jnp.zeros_like(acc_scr)

    def full_block(t, carry):
        key_block(t, False, False)
        return carry

    @pl.when(i == 0)
    def _():
        produce_diag(0, 0, i)
        key_block(i, True, None)

    @pl.when(i > 0)
    def _():
        produce_full(0, 0, 0)
        lax.fori_loop(0, i - 1, full_block, 0)
        key_block(i - 1, False, True)
        key_block(i, True, None)


def _attention(q, k, vt, meta_kv, *, batch, seq, tile):
    rows = batch * seq
    nq = seq // tile
    has_meta = meta_kv is not None
    assert (N_HEADS // HEAD_GROUP) % 2 == 0 and HEAD_GROUP % 2 == 0
    diag = min(tile, DIAG_BLOCK)
    idx = np.arange(diag)
    tri = np.where(idx[:, None] <= idx[None, :], 0.0, MASK_VALUE).astype(np.float32)
    in_specs = [
        pl.BlockSpec((N_HEADS * HEAD_PAD, tile), lambda b, i: (0, b * nq + i)),
        pl.BlockSpec((N_HEADS, seq, HEAD_PAD), lambda b, i: (0, b, 0)),
        pl.BlockSpec((nq, ATT_WIDTH, tile), lambda b, i: (b, 0, 0)),
        _const_spec((diag, diag)),
    ]
    args = [q, k, vt, tri]
    if has_meta:
        km, vtm = meta_kv
        in_specs += [_const_spec(km.shape), _const_spec(vtm.shape)]
        args += [km, vtm]
    body = functools.partial(_attention_body, tile=tile, diag=diag, has_meta=has_meta,
                             group=HEAD_GROUP)
    return pl.pallas_call(
        body,
        grid=(batch, nq),
        in_specs=in_specs,
        out_specs=pl.BlockSpec((tile, ATT_WIDTH), lambda b, i: (b * nq + i, 0)),
        out_shape=jax.ShapeDtypeStruct((rows, ATT_WIDTH), jnp.bfloat16),
        scratch_shapes=[
            pltpu.VMEM((2, HEAD_GROUP, tile + N_META, tile + SCORE_PITCH_PAD), jnp.float32),
            pltpu.VMEM((2, HEAD_GROUP, 1, tile), jnp.float32),
            pltpu.VMEM((N_HEADS, 1, tile), jnp.float32),
            pltpu.VMEM((N_HEADS, V_DIM + ONES_ROWS, tile), jnp.float32),
        ],
        compiler_params=pltpu.CompilerParams(
            dimension_semantics=("parallel", "parallel"), vmem_limit_bytes=VMEM_LIMIT),
        name="attention",
    )(*args)


def _post_attention_body(h_ref, a_ref, b_ref, gmix_ref, wgates_ref, wpa_ref, wpb_ref, wo_ref,
                         gffn_ref, wgate_ref, wup_ref, wdown_ref, gfin_ref, o_ref, *, final_norm):
    tile = h_ref.shape[0]
    step = min(tile, max(tile // POST_SPLIT, MXU_ROWS))
    parts = [pl.ds(r * step, step) for r in range(tile // step)]
    pa = [_dot(a_ref[r, :], wpa_ref[...]) for r in parts]
    pb = [_dot(b_ref[r, :], wpb_ref[...]) for r in parts]
    hs = [h_ref[r, :] for r in parts]
    hn = [_rmsnorm(h, gmix_ref[...]).astype(jnp.bfloat16) for h in hs]
    gates = [jax.nn.sigmoid(_dot(x, wgates_ref[...])) for x in hn]
    merged = [(g[:, :D_MODEL] * xa + g[:, D_MODEL:] * xb).astype(jnp.bfloat16)
              for g, xa, xb in zip(gates, pa, pb)]
    hs = [h + _dot(m, wo_ref[...]) for h, m in zip(hs, merged)]
    hn = [_rmsnorm(h, gffn_ref[...]).astype(jnp.bfloat16) for h in hs]
    act = [(jax.nn.silu(_dot(x, wgate_ref[...])) * _dot(x, wup_ref[...])).astype(jnp.bfloat16)
           for x in hn]
    hs = [h + _dot(x, wdown_ref[...]) for h, x in zip(hs, act)]
    for r, h in zip(parts, hs):
        o_ref[r, :] = _rmsnorm(h, gfin_ref[...]) if final_norm else h


def _post_attention(h, a, b, lw, layer, g_final, *, tile, final_norm):
    rows = h.shape[0]
    row_map = lambda r: (r, 0)
    body = functools.partial(_post_attention_body, final_norm=final_norm)
    return pl.pallas_call(
        body,
        grid=(rows // tile,),
        in_specs=[
            pl.BlockSpec((tile, D_MODEL), row_map),
            pl.BlockSpec((tile, POOL_WIDTH), row_map),
            pl.BlockSpec((tile, ATT_WIDTH), row_map),
            _const_spec((1, D_MODEL), layer),
            _const_spec((D_MODEL, 2 * D_MODEL), layer),
            _const_spec((POOL_WIDTH, D_MODEL), layer),
            _const_spec((ATT_WIDTH, D_MODEL), layer),
            _const_spec((D_MODEL, D_MODEL), layer),
            _const_spec((1, D_MODEL), layer),
            _const_spec((D_MODEL, D_FF), layer),
            _const_spec((D_MODEL, D_FF), layer),
            _const_spec((D_FF, D_MODEL), layer),
            _const_spec((1, D_MODEL)),
        ],
        out_specs=pl.BlockSpec((tile, D_MODEL), row_map),
        out_shape=jax.ShapeDtypeStruct((rows, D_MODEL), jnp.float32),
        compiler_params=pltpu.CompilerParams(
            dimension_semantics=("parallel",), vmem_limit_bytes=VMEM_LIMIT),
        name="post_attention",
    )(h, a, b, lw["g_mix"], lw["w_in_gates"], lw["w_pa"], lw["w_pb"], lw["w_o"], lw["g_ffn"],
      lw["w_gate"], lw["w_up"], lw["w_down"], g_final)


def _head_lanes():
    return [(QK_NOPE, QK_NOPE + ROPE_HALF), (QK_NOPE + ROPE_HALF, QK_DIM), (0, QK_NOPE),
            (None, HEAD_PAD - QK_DIM)]


def _place_head_columns(w, keep, xp=jnp):
    parts = []
    for start, stop in _head_lanes():
        if start is None or not keep(start):
            width = stop if start is None else stop - start
            parts.append(xp.zeros(w.shape[:-1] + (width,), w.dtype))
        else:
            parts.append(w[..., start:stop])
    return xp.concatenate(parts, axis=-1)


def _prepare_weights(norm_mix_g, w_in, pool_w, pool_scale, q_norm_g, kv_norm_g, w_uq, w_ukv,
                     w_pa, w_pb, w_o, norm_ffn_g, w_gate, w_up, w_down):
    bf = jnp.bfloat16
    depth = w_in.shape[0]
    o_q = POOL_WIDTH + Q_RANK
    o_kv = o_q + KV_RANK
    o_kr = o_kv + QK_ROPE
    is_nope = lambda col: col < QK_NOPE
    kr_cols = _place_head_columns(jnp.pad(w_in[..., o_kv:o_kr], ((0, 0), (0, 0), (QK_NOPE, 0))),
                                  lambda col: not is_nope(col))
    w_in_a = jnp.concatenate([w_in[..., :o_kv], kr_cols], axis=-1).astype(bf)
    w_q = _place_head_columns(w_uq.reshape(depth, Q_RANK, N_HEADS, QK_DIM), lambda col: True)
    ukv = w_ukv.reshape(depth, KV_RANK, N_HEADS, QK_NOPE + V_DIM)
    w_k = _place_head_columns(jnp.pad(ukv[..., :QK_NOPE], [(0, 0)] * 3 + [(0, QK_ROPE)]), is_nope)
    w_vt = jnp.swapaxes(ukv[..., QK_NOPE:].reshape(depth, KV_RANK, ATT_WIDTH), 1, 2)
    row = lambda g: g[:, None, :]
    return dict(
        g_mix=row(norm_mix_g), w_in_a=w_in_a, w_in_gates=w_in[..., o_kr:].astype(bf),
        pool_w=pool_w.astype(bf), pool_scale=row(pool_scale),
        q_g=row(q_norm_g), kv_g=row(kv_norm_g),
        w_qt=jnp.swapaxes(w_q.reshape(depth, Q_RANK, N_HEADS * HEAD_PAD), 1, 2).astype(bf),
        w_k=w_k.reshape(depth, KV_RANK, N_HEADS * HEAD_PAD).astype(bf), w_vt=w_vt.astype(bf),
        w_pa=w_pa.astype(bf), w_pb=w_pb.astype(bf), w_o=w_o.astype(bf),
        g_ffn=row(norm_ffn_g), w_gate=w_gate.astype(bf), w_up=w_up.astype(bf),
        w_down=w_down.astype(bf))


def _rope_tables(length):
    f32 = np.float32
    inv = f32(1.0) / (f32(ROPE_THETA) ** (np.arange(0, QK_ROPE, 2, dtype=f32) / f32(QK_ROPE)))
    ang = np.arange(length, dtype=f32)[:, None] * inv[None, :]
    cos, sin = np.cos(ang).astype(f32), np.sin(ang).astype(f32)
    ones = np.ones((length, QK_NOPE), f32)
    cos_t = np.asarray(_place_head_columns(np.concatenate([ones, cos, cos], axis=1),
                                           lambda col: True, xp=np))
    sin_t = np.asarray(_place_head_columns(np.concatenate([0 * ones, -sin, sin], axis=1),
                                           lambda col: True, xp=np))
    return cos_t * f32(Q_SCALE), sin_t * f32(Q_SCALE), cos_t, sin_t


def _tile_q_tables(tables, tile):
    cos_q, sin_q, cos_k, sin_k = tables
    per_tile = lambda t: np.ascontiguousarray(t.reshape(-1, tile, HEAD_PAD).transpose(0, 2, 1))
    return per_tile(cos_q), per_tile(sin_q), cos_k, sin_k


def kernel(x, meta_tokens, norm_mix_g, w_in, pool_w, pool_scale, q_norm_g, kv_norm_g, w_uq, w_ukv,
           w_pa, w_pb, w_o, norm_ffn_g, w_gate, w_up, w_down, final_norm_g):
    batch, seq, _ = x.shape
    depth = w_in.shape[0]
    tile = 512
    tables = _rope_tables(N_META + max(seq, META_TILE))
    meta_tables = _tile_q_tables(tuple(t[:META_TILE] for t in tables), META_TILE)
    main_tables = _tile_q_tables(tuple(t[N_META:N_META + seq] for t in tables), tile)
    g_final = final_norm_g[None]
    lw = _prepare_weights(norm_mix_g, w_in, pool_w, pool_scale, q_norm_g, kv_norm_g, w_uq, w_ukv,
                          w_pa, w_pb, w_o, norm_ffn_g, w_gate, w_up, w_down)

    h = x.reshape(batch * seq, D_MODEL)
    hm = jnp.pad(meta_tokens.astype(x.dtype), ((0, META_TILE - N_META), (0, 0)))
    zero_halo = np.zeros((MAX_WINDOW, POOL_WIDTH), np.float32)
    for i in range(depth):
        last = i == depth - 1
        qm, km, vtm, am, um = _pre_attention(hm, lw, i, meta_tables, zero_halo, batch=1,
                                             seq=META_TILE, tile=META_TILE, pos_offset=0,
                                             emit_u=True)
        if not last:
            bm = _attention(qm, km, vtm, None, batch=1, seq=META_TILE, tile=META_TILE)
            hm = _post_attention(hm, am, bm, lw, i, g_final, tile=META_TILE, final_norm=False)
        q, k, vt, a = _pre_attention(h, lw, i, main_tables, um, batch=batch, seq=seq, tile=tile,
                                     pos_offset=N_META, emit_u=False)
        b = _attention(q, k, vt, (km, vtm), batch=batch, seq=seq, tile=tile)
        h = _post_attention(h, a, b, lw, i, g_final, tile=tile, final_norm=last)
    return h.reshape(batch, seq, D_MODEL)
```

```python
import functools
import math

import jax
import jax.numpy as jnp
import numpy as np
from jax import lax
from jax.experimental import pallas as pl
from jax.experimental.pallas import tpu as pltpu

D_MODEL = 1024
N_META = 16
POOL_WINDOWS = (2, 4, 8, 16)
POOL_GROUP = 128
POOL_WIDTH = POOL_GROUP * len(POOL_WINDOWS)
N_HEADS = 16
QK_NOPE = 64
QK_ROPE = 32
V_DIM = 64
Q_RANK = 256
KV_RANK = 128
QK_DIM = QK_NOPE + QK_ROPE
ROPE_HALF = QK_ROPE // 2
ATT_WIDTH = N_HEADS * V_DIM
ROPE_THETA = 10000.0
D_FF = 2816
NORM_EPS = 1e-6
MASK_VALUE = -1e30

LANES = 128
HEAD_PAD = LANES
MAX_WINDOW = max(POOL_WINDOWS)
assert all(w & (w - 1) == 0 for w in POOL_WINDOWS)
POOL_PAD = 8
Q_SCALE = (QK_DIM ** -0.5) * math.log2(math.e)
META_TILE = 128
HEAD_GROUP = 2
ONES_ROWS = 16
POST_SPLIT = 2
MXU_ROWS = 256
ATTN_TILES = 2
SCORE_PITCH_PAD = 128
DIAG_BLOCK = 256
VMEM_LIMIT = 56 * 1024 * 1024

_NT = (((1,), (1,)), ((), ()))


def _rmsnorm(x, g):
    return x * lax.rsqrt(jnp.mean(x * x, axis=-1, keepdims=True) + NORM_EPS) * g


def _dot(a, b):
    return jnp.dot(a, b, preferred_element_type=jnp.float32)


def _const_spec(shape, layer=None):
    zeros = (0,) * len(shape)
    if layer is None:
        return pl.BlockSpec(shape, lambda *_: zeros, pipeline_mode=pl.Buffered(1))
    return pl.BlockSpec((None,) + tuple(shape), lambda *_: (layer,) + zeros,
                        pipeline_mode=pl.Buffered(1))


def _pre_attention_body(h_ref, gmix_ref, win_ref, poolw_ref, pscale_ref, qg_ref, kvg_ref,
                        wqt_ref, wk_ref, wvt_ref, cosq_ref, sinq_ref, cosk_ref, sink_ref, halo_ref,
                        q_out, k_out, vt_out, a_out, *rest, tile, pos_offset, emit_u):
    if emit_u:
        u_out, pool_scr = rest
    else:
        (pool_scr,) = rest
    j = pl.program_id(1)
    row0 = pl.multiple_of(j * tile, tile)
    rows = pl.ds(row0, tile)
    first, end = POOL_PAD + MAX_WINDOW, POOL_PAD + MAX_WINDOW + tile

    @pl.when(j == 0)
    def _():
        pool_scr[:, 0:POOL_PAD, :] = jnp.zeros((3, POOL_PAD, POOL_WIDTH), jnp.float32)
        pool_scr[0, POOL_PAD:first, :] = halo_ref[...]

    hn = _rmsnorm(h_ref[...], gmix_ref[...]).astype(jnp.bfloat16)
    z = _dot(hn, win_ref[:, POOL_WIDTH:])
    u = _dot(hn, win_ref[:, 0:POOL_WIDTH])

    c_q = z[:, 0:Q_RANK]
    c_kv = z[:, Q_RANK:Q_RANK + KV_RANK]
    kr = z[:, Q_RANK + KV_RANK:]

    cqn = _rmsnorm(c_q, qg_ref[...]).astype(jnp.bfloat16)
    qt_raw = lax.dot_general(wqt_ref[...], cqn, _NT, preferred_element_type=jnp.float32)
    cos_q, sin_q = cosq_ref[j], sinq_ref[j]
    for hd in range(N_HEADS):
        x = qt_raw[hd * HEAD_PAD:(hd + 1) * HEAD_PAD, :]
        partner = jnp.concatenate([x[ROPE_HALF:QK_ROPE], x[:ROPE_HALF], x[QK_ROPE:]], axis=0)
        q_out[hd * HEAD_PAD:(hd + 1) * HEAD_PAD, :] = (x * cos_q + partner * sin_q).astype(jnp.bfloat16)

    ckvn = _rmsnorm(c_kv, kvg_ref[...]).astype(jnp.bfloat16)
    k_raw = _dot(ckvn, wk_ref[...])
    lane = lax.broadcasted_iota(jnp.int32, kr.shape, 1)
    kr_partner = jnp.where(lane < ROPE_HALF, pltpu.roll(kr, LANES - ROPE_HALF, 1),
                           pltpu.roll(kr, ROPE_HALF, 1))
    kr_roped = kr * cosk_ref[rows, :] + kr_partner * sink_ref[rows, :]
    for hd in range(N_HEADS):
        k_out[hd] = (k_raw[:, hd * HEAD_PAD:(hd + 1) * HEAD_PAD] + kr_roped).astype(jnp.bfloat16)
    vt = lax.dot_general(wvt_ref[...], ckvn, _NT, preferred_element_type=jnp.float32)
    vt_out[0] = vt.astype(jnp.bfloat16)

    pool_scr[0, first:end, :] = u
    if emit_u:
        u_out[...] = u
    g1, g2, g3 = POOL_GROUP, 2 * POOL_GROUP, 3 * POOL_GROUP
    pool_scr[1, POOL_PAD:end, :] = pool_scr[0, POOL_PAD:end, :] + pool_scr[0, POOL_PAD - 1:end - 1, :]
    pool_scr[2, POOL_PAD:end, g1:] = (pool_scr[1, POOL_PAD:end, g1:]
                                      + pool_scr[1, POOL_PAD - 2:end - 2, g1:])
    pool_scr[1, POOL_PAD:end, g2:] = (pool_scr[2, POOL_PAD:end, g2:]
                                      + pool_scr[2, POOL_PAD - 4:end - 4, g2:])
    wsums = [pool_scr[1, first:end, 0:g1], pool_scr[2, first:end, g1:g2],
             pool_scr[1, first:end, g2:g3],
             pool_scr[1, first:end, g3:] + pool_scr[1, first - 8:end - 8, g3:]]
    outs = []
    for g, w in enumerate(POOL_WINDOWS):
        if pos_offset + 1 >= MAX_WINDOW:
            mean = wsums[g] * (1.0 / w)
        else:
            pos = pos_offset + row0 + lax.broadcasted_iota(jnp.int32, (tile, POOL_GROUP), 0)
            mean = wsums[g] / jnp.minimum(pos + 1, w).astype(jnp.float32)
        y = (mean - u[:, g * POOL_GROUP:(g + 1) * POOL_GROUP]).astype(jnp.bfloat16)
        outs.append(_dot(y, poolw_ref[g]))
    a = jnp.concatenate(outs, axis=1) * pscale_ref[...]
    a_out[...] = a.astype(jnp.bfloat16)
    pool_scr[0, POOL_PAD:first, :] = pool_scr[0, end - MAX_WINDOW:end, :]


def _pre_attention(h, lw, layer, tables, halo, *, batch, seq, tile, pos_offset, emit_u):
    rows = batch * seq
    n_tiles = seq // tile
    row_map = lambda b, j: (b * n_tiles + j, 0)
    in_specs = [
        pl.BlockSpec((tile, D_MODEL), row_map),
        _const_spec((1, D_MODEL), layer),
        _const_spec((D_MODEL, 1024), layer),
        _const_spec((len(POOL_WINDOWS), POOL_GROUP, POOL_GROUP), layer),
        _const_spec((1, POOL_WIDTH), layer),
        _const_spec((1, Q_RANK), layer),
        _const_spec((1, KV_RANK), layer),
        _const_spec((N_HEADS * HEAD_PAD, Q_RANK), layer),
        _const_spec((KV_RANK, N_HEADS * HEAD_PAD), layer),
        _const_spec((ATT_WIDTH, KV_RANK), layer),
        _const_spec((n_tiles, HEAD_PAD, tile)),
        _const_spec((n_tiles, HEAD_PAD, tile)),
        _const_spec((seq, LANES)),
        _const_spec((seq, LANES)),
        pl.BlockSpec((MAX_WINDOW, POOL_WIDTH), lambda b, j: (0, 0)),
    ]
    out_shape = [
        jax.ShapeDtypeStruct((N_HEADS * HEAD_PAD, rows), jnp.bfloat16),
        jax.ShapeDtypeStruct((N_HEADS, rows, HEAD_PAD), jnp.bfloat16),
        jax.ShapeDtypeStruct((rows // tile, ATT_WIDTH, tile), jnp.bfloat16),
        jax.ShapeDtypeStruct((rows, POOL_WIDTH), jnp.bfloat16),
    ]
    out_specs = [
        pl.BlockSpec((N_HEADS * HEAD_PAD, tile), lambda b, j: (0, b * n_tiles + j)),
        pl.BlockSpec((N_HEADS, tile, HEAD_PAD), lambda b, j: (0, b * n_tiles + j, 0)),
        pl.BlockSpec((1, ATT_WIDTH, tile), lambda b, j: (b * n_tiles + j, 0, 0)),
        pl.BlockSpec((tile, POOL_WIDTH), row_map),
    ]
    if emit_u:
        out_shape.append(jax.ShapeDtypeStruct((rows, POOL_WIDTH), jnp.float32))
        out_specs.append(pl.BlockSpec((tile, POOL_WIDTH), row_map))
    body = functools.partial(_pre_attention_body, tile=tile, pos_offset=pos_offset, emit_u=emit_u)
    return pl.pallas_call(
        body,
        grid=(batch, n_tiles),
        in_specs=in_specs,
        out_specs=out_specs,
        out_shape=out_shape,
        scratch_shapes=[pltpu.VMEM((3, POOL_PAD + MAX_WINDOW + tile, POOL_WIDTH), jnp.float32)],
        compiler_params=pltpu.CompilerParams(
            dimension_semantics=("arbitrary", "arbitrary"), vmem_limit_bytes=VMEM_LIMIT),
        name="pre_attention",
    )(h, lw["g_mix"], lw["w_in_a"], lw["pool_w"], lw["pool_scale"], lw["q_g"], lw["kv_g"],
      lw["w_qt"], lw["w_k"], lw["w_vt"], *tables, halo)


def _attention_body(*refs, tile, tiles, diag, has_meta, group):
    if has_meta:
        q_ref, k_ref, vt_ref, tri_ref, km_ref, vtm_ref, o_ref, s_scr, smax_scr, m_scr, acc_scr = refs
    else:
        q_ref, k_ref, vt_ref, tri_ref, o_ref, s_scr, smax_scr, m_scr, acc_scr = refs
    step = pl.program_id(1)
    n_groups = N_HEADS // group
    n_sub = tile // diag

    def q_block(qt, hd, lo=0):
        return q_ref[hd * HEAD_PAD:(hd + 1) * HEAD_PAD, qt * tile + lo:(qt + 1) * tile]

    def produce_full(slot, qt, hg, t):
        k_rows = pl.ds(pl.multiple_of(t * tile, tile), tile)
        for g in range(group):
            hd = hg * group + g
            s = _dot(k_ref[hd, k_rows, :], q_block(qt, hd))
            s_scr[slot, g, 0:tile, 0:tile] = s
            smax_scr[slot, g] = jnp.max(s, axis=0, keepdims=True)

    def produce_diag(slot, qt, hg, t):
        for g in range(group):
            hd = hg * group + g
            smax = None
            for c in range(n_sub):
                lo = c * diag
                k = k_ref[hd, pl.ds(pl.multiple_of(t * tile + lo, diag), diag), :]
                if c == 0 and has_meta:
                    k = jnp.concatenate([k, km_ref[hd, 0:N_META, :]], axis=0)
                s = _dot(k, q_block(qt, hd, lo))
                on_diag = s[0:diag, 0:diag] + tri_ref[...]
                main = on_diag if lo + diag == tile else jnp.concatenate(
                    [on_diag, s[0:diag, diag:]], axis=1)
                s_scr[slot, g, lo:lo + diag, lo:tile] = main
                cmax = jnp.max(main, axis=0, keepdims=True)
                if c == 0 and has_meta:
                    s_scr[slot, g, tile:tile + N_META, 0:tile] = s[diag:, :]
                    cmax = jnp.maximum(cmax, jnp.max(s[diag:, :], axis=0, keepdims=True))
                smax = cmax if c == 0 else jnp.concatenate(
                    [smax[:, :lo], jnp.maximum(smax[:, lo:], cmax)], axis=1)
            smax_scr[slot, g] = smax

    def consume(slot, qt, hg, t, is_diag):
        for g in range(group):
            hd = hg * group + g
            state = qt * N_HEADS + hd
            v_rows = pl.ds(pl.multiple_of(hd * V_DIM, V_DIM), V_DIM)
            m_old = m_scr[state]
            m_new = jnp.maximum(m_old, smax_scr[slot, g])
            if not is_diag:
                p = jnp.exp2(s_scr[slot, g, 0:tile, 0:tile] - m_new).astype(jnp.bfloat16)
                ones = jnp.ones((ONES_ROWS, tile), jnp.bfloat16)
                pv = _dot(jnp.concatenate([vt_ref[t, v_rows, :], ones], axis=0), p)
            else:
                ones = jnp.ones((ONES_ROWS, diag), jnp.bfloat16)
                for c in range(n_sub):
                    lo = c * diag
                    p = jnp.exp2(s_scr[slot, g, lo:lo + diag, lo:tile] - m_new[:, lo:]).astype(jnp.bfloat16)
                    part = _dot(jnp.concatenate([vt_ref[t, v_rows, lo:lo + diag], ones], axis=0), p)
                    pv = part if c == 0 else jnp.concatenate(
                        [pv[:, :lo], pv[:, lo:] + part], axis=1)
                if has_meta:
                    pm = jnp.exp2(s_scr[slot, g, tile:tile + N_META, 0:tile] - m_new).astype(jnp.bfloat16)
                    ones_meta = jnp.ones((ONES_ROWS, N_META), jnp.bfloat16)
                    pv = pv + _dot(
                        jnp.concatenate([vtm_ref[0, v_rows, 0:N_META], ones_meta], axis=0), pm)
            acc_scr[state] = jnp.exp2(m_old - m_new) * acc_scr[state] + pv
            m_scr[state] = m_new

    def finalize_pair(qt, pair):
        acc0, acc1 = acc_scr[qt * N_HEADS + 2 * pair], acc_scr[qt * N_HEADS + 2 * pair + 1]
        both = jnp.concatenate([acc0[0:V_DIM] / acc0[V_DIM:V_DIM + 1],
                                acc1[0:V_DIM] / acc1[V_DIM:V_DIM + 1]], axis=0)
        o_ref[qt * tile:(qt + 1) * tile, pair * LANES:(pair + 1) * LANES] = both.T.astype(jnp.bfloat16)

    def key_block(qt, t, is_diag, produce_next):
        for hg in range(n_groups):
            if hg + 1 < n_groups:
                (produce_diag if is_diag else produce_full)((hg + 1) % 2, qt, hg + 1, t)
            elif produce_next is not None:
                produce_next()
            consume(hg % 2, qt, hg, t, is_diag)
            if is_diag:
                for hd in range(hg * group, (hg + 1) * group, 2):
                    finalize_pair(qt, hd // 2)

    m_scr[...] = jnp.full_like(m_scr, MASK_VALUE)
    acc_scr[...] = jnp.zeros_like(acc_scr)

    for qt in range(tiles):
        i = step * tiles + qt
        first_of_next = (lambda qt=qt: produce_full(0, qt + 1, 0, 0)) if qt + 1 < tiles else None

        def full_blocks_then_diag(qt=qt, i=i, first_of_next=first_of_next):
            def full_block(t, carry):
                key_block(qt, t, False, lambda: produce_full(0, qt, 0, t + 1))
                return carry
            lax.fori_loop(0, i - 1, full_block, 0)
            key_block(qt, i - 1, False, lambda: produce_diag(0, qt, 0, i))
            key_block(qt, i, True, first_of_next)

        if qt > 0:
            full_blocks_then_diag()
        else:
            @pl.when(i == 0)
            def _(qt=qt, i=i, first_of_next=first_of_next):
                produce_diag(0, qt, 0, i)
                key_block(qt, i, True, first_of_next)

            @pl.when(i > 0)
            def _(qt=qt, run=full_blocks_then_diag):
                produce_full(0, qt, 0, 0)
                run()


def _attention(q, k, vt, meta_kv, *, batch, seq, tile, tiles):
    rows = batch * seq
    nq = seq // tile
    steps = nq // tiles
    has_meta = meta_kv is not None
    assert (N_HEADS // HEAD_GROUP) % 2 == 0 and HEAD_GROUP % 2 == 0 and nq % tiles == 0
    diag = min(tile, DIAG_BLOCK)
    idx = np.arange(diag)
    tri = np.where(idx[:, None] <= idx[None, :], 0.0, MASK_VALUE).astype(np.float32)
    in_specs = [
        pl.BlockSpec((N_HEADS * HEAD_PAD, tiles * tile), lambda b, i: (0, b * steps + i)),
        pl.BlockSpec((N_HEADS, seq, HEAD_PAD), lambda b, i: (0, b, 0)),
        pl.BlockSpec((nq, ATT_WIDTH, tile), lambda b, i: (b, 0, 0)),
        _const_spec((diag, diag)),
    ]
    args = [q, k, vt, tri]
    if has_meta:
        km, vtm = meta_kv
        in_specs += [_const_spec(km.shape), _const_spec(vtm.shape)]
        args += [km, vtm]
    body = functools.partial(_attention_body, tile=tile, tiles=tiles, diag=diag, has_meta=has_meta,
                             group=HEAD_GROUP)
    return pl.pallas_call(
        body,
        grid=(batch, steps),
        in_specs=in_specs,
        out_specs=pl.BlockSpec((tiles * tile, ATT_WIDTH), lambda b, i: (b * steps + i, 0)),
        out_shape=jax.ShapeDtypeStruct((rows, ATT_WIDTH), jnp.bfloat16),
        scratch_shapes=[
            pltpu.VMEM((2, HEAD_GROUP, tile + N_META, tile + SCORE_PITCH_PAD), jnp.float32),
            pltpu.VMEM((2, HEAD_GROUP, 1, tile), jnp.float32),
            pltpu.VMEM((tiles * N_HEADS, 1, tile), jnp.float32),
            pltpu.VMEM((tiles * N_HEADS, V_DIM + ONES_ROWS, tile), jnp.float32),
        ],
        compiler_params=pltpu.CompilerParams(
            dimension_semantics=("parallel", "parallel"), vmem_limit_bytes=VMEM_LIMIT),
        name="attention",
    )(*args)


def _post_attention_body(h_ref, a_ref, b_ref, gmix_ref, wgates_ref, wpa_ref, wpb_ref, wo_ref,
                         gffn_ref, wgate_ref, wup_ref, wdown_ref, gfin_ref, o_ref, *, final_norm):
    tile = h_ref.shape[0]
    step = min(tile, max(tile // POST_SPLIT, MXU_ROWS))
    parts = [pl.ds(r * step, step) for r in range(tile // step)]
    pa = [_dot(a_ref[r, :], wpa_ref[...]) for r in parts]
    pb = [_dot(b_ref[r, :], wpb_ref[...]) for r in parts]
    hs = [h_ref[r, :] for r in parts]
    hn = [_rmsnorm(h, gmix_ref[...]).astype(jnp.bfloat16) for h in hs]
    gates = [jax.nn.sigmoid(_dot(x, wgates_ref[...])) for x in hn]
    merged = [(g[:, :D_MODEL] * xa + g[:, D_MODEL:] * xb).astype(jnp.bfloat16)
              for g, xa, xb in zip(gates, pa, pb)]
    hs = [h + _dot(m, wo_ref[...]) for h, m in zip(hs, merged)]
    hn = [_rmsnorm(h, gffn_ref[...]).astype(jnp.bfloat16) for h in hs]
    act = [(jax.nn.silu(_dot(x, wgate_ref[...])) * _dot(x, wup_ref[...])).astype(jnp.bfloat16)
           for x in hn]
    hs = [h + _dot(x, wdown_ref[...]) for h, x in zip(hs, act)]
    for r, h in zip(parts, hs):
        o_ref[r, :] = _rmsnorm(h, gfin_ref[...]) if final_norm else h


def _post_attention(h, a, b, lw, layer, g_final, *, tile, final_norm):
    rows = h.shape[0]
    row_map = lambda r: (r, 0)
    body = functools.partial(_post_attention_body, final_norm=final_norm)
    return pl.pallas_call(
        body,
        grid=(rows // tile,),
        in_specs=[
            pl.BlockSpec((tile, D_MODEL), row_map),
            pl.BlockSpec((tile, POOL_WIDTH), row_map),
            pl.BlockSpec((tile, ATT_WIDTH), row_map),
            _const_spec((1, D_MODEL), layer),
            _const_spec((D_MODEL, 2 * D_MODEL), layer),
            _const_spec((POOL_WIDTH, D_MODEL), layer),
            _const_spec((ATT_WIDTH, D_MODEL), layer),
            _const_spec((D_MODEL, D_MODEL), layer),
            _const_spec((1, D_MODEL), layer),
            _const_spec((D_MODEL, D_FF), layer),
            _const_spec((D_MODEL, D_FF), layer),
            _const_spec((D_FF, D_MODEL), layer),
            _const_spec((1, D_MODEL)),
        ],
        out_specs=pl.BlockSpec((tile, D_MODEL), row_map),
        out_shape=jax.ShapeDtypeStruct((rows, D_MODEL), jnp.float32),
        compiler_params=pltpu.CompilerParams(
            dimension_semantics=("parallel",), vmem_limit_bytes=VMEM_LIMIT),
        name="post_attention",
    )(h, a, b, lw["g_mix"], lw["w_in_gates"], lw["w_pa"], lw["w_pb"], lw["w_o"], lw["g_ffn"],
      lw["w_gate"], lw["w_up"], lw["w_down"], g_final)


def _head_lanes():
    return [(QK_NOPE, QK_NOPE + ROPE_HALF), (QK_NOPE + ROPE_HALF, QK_DIM), (0, QK_NOPE),
            (None, HEAD_PAD - QK_DIM)]


def _place_head_columns(w, keep, xp=jnp):
    parts = []
    for start, stop in _head_lanes():
        if start is None or not keep(start):
            width = stop if start is None else stop - start
            parts.append(xp.zeros(w.shape[:-1] + (width,), w.dtype))
        else:
            parts.append(w[..., start:stop])
    return xp.concatenate(parts, axis=-1)


def _prepare_weights(norm_mix_g, w_in, pool_w, pool_scale, q_norm_g, kv_norm_g, w_uq, w_ukv,
                     w_pa, w_pb, w_o, norm_ffn_g, w_gate, w_up, w_down):
    bf = jnp.bfloat16
    depth = w_in.shape[0]
    o_q = POOL_WIDTH + Q_RANK
    o_kv = o_q + KV_RANK
    o_kr = o_kv + QK_ROPE
    is_nope = lambda col: col < QK_NOPE
    kr_cols = _place_head_columns(jnp.pad(w_in[..., o_kv:o_kr], ((0, 0), (0, 0), (QK_NOPE, 0))),
                                  lambda col: not is_nope(col))
    w_in_a = jnp.concatenate([w_in[..., :o_kv], kr_cols], axis=-1).astype(bf)
    w_q = _place_head_columns(w_uq.reshape(depth, Q_RANK, N_HEADS, QK_DIM), lambda col: True)
    ukv = w_ukv.reshape(depth, KV_RANK, N_HEADS, QK_NOPE + V_DIM)
    w_k = _place_head_columns(jnp.pad(ukv[..., :QK_NOPE], [(0, 0)] * 3 + [(0, QK_ROPE)]), is_nope)
    w_vt = jnp.swapaxes(ukv[..., QK_NOPE:].reshape(depth, KV_RANK, ATT_WIDTH), 1, 2)
    row = lambda g: g[:, None, :]
    return dict(
        g_mix=row(norm_mix_g), w_in_a=w_in_a, w_in_gates=w_in[..., o_kr:].astype(bf),
        pool_w=pool_w.astype(bf), pool_scale=row(pool_scale),
        q_g=row(q_norm_g), kv_g=row(kv_norm_g),
        w_qt=jnp.swapaxes(w_q.reshape(depth, Q_RANK, N_HEADS * HEAD_PAD), 1, 2).astype(bf),
        w_k=w_k.reshape(depth, KV_RANK, N_HEADS * HEAD_PAD).astype(bf), w_vt=w_vt.astype(bf),
        w_pa=w_pa.astype(bf), w_pb=w_pb.astype(bf), w_o=w_o.astype(bf),
        g_ffn=row(norm_ffn_g), w_gate=w_gate.astype(bf), w_up=w_up.astype(bf),
        w_down=w_down.astype(bf))


def _rope_tables(length):
    f32 = np.float32
    inv = f32(1.0) / (f32(ROPE_THETA) ** (np.arange(0, QK_ROPE, 2, dtype=f32) / f32(QK_ROPE)))
    ang = np.arange(length, dtype=f32)[:, None] * inv[None, :]
    cos, sin = np.cos(ang).astype(f32), np.sin(ang).astype(f32)
    ones = np.ones((length, QK_NOPE), f32)
    cos_t = np.asarray(_place_head_columns(np.concatenate([ones, cos, cos], axis=1),
                                           lambda col: True, xp=np))
    sin_t = np.asarray(_place_head_columns(np.concatenate([0 * ones, -sin, sin], axis=1),
                                           lambda col: True, xp=np))
    return cos_t * f32(Q_SCALE), sin_t * f32(Q_SCALE), cos_t, sin_t


def _tile_q_tables(tables, tile):
    cos_q, sin_q, cos_k, sin_k = tables
    per_tile = lambda t: np.ascontiguousarray(t.reshape(-1, tile, HEAD_PAD).transpose(0, 2, 1))
    return per_tile(cos_q), per_tile(sin_q), cos_k, sin_k


def kernel(x, meta_tokens, norm_mix_g, w_in, pool_w, pool_scale, q_norm_g, kv_norm_g, w_uq, w_ukv,
           w_pa, w_pb, w_o, norm_ffn_g, w_gate, w_up, w_down, final_norm_g):
    batch, seq, _ = x.shape
    depth = w_in.shape[0]
    tile = 512
    tables = _rope_tables(N_META + max(seq, META_TILE))
    meta_tables = _tile_q_tables(tuple(t[:META_TILE] for t in tables), META_TILE)
    main_tables = _tile_q_tables(tuple(t[N_META:N_META + seq] for t in tables), tile)
    g_final = final_norm_g[None]
    lw = _prepare_weights(norm_mix_g, w_in, pool_w, pool_scale, q_norm_g, kv_norm_g, w_uq, w_ukv,
                          w_pa, w_pb, w_o, norm_ffn_g, w_gate, w_up, w_down)

    h = x.reshape(batch * seq, D_MODEL)
    hm = jnp.pad(meta_tokens.astype(x.dtype), ((0, META_TILE - N_META), (0, 0)))
    zero_halo = np.zeros((MAX_WINDOW, POOL_WIDTH), np.float32)
    for i in range(depth):
        last = i == depth - 1
        qm, km, vtm, am, um = _pre_attention(hm, lw, i, meta_tables, zero_halo, batch=1,
                                             seq=META_TILE, tile=META_TILE, pos_offset=0,
                                             emit_u=True)
        if not last:
            bm = _attention(qm, km, vtm, None, batch=1, seq=META_TILE, tile=META_TILE, tiles=1)
            hm = _post_attention(hm, am, bm, lw, i, g_final, tile=META_TILE, final_norm=False)
        q, k, vt, a = _pre_attention(h, lw, i, main_tables, um, batch=batch, seq=seq, tile=tile,
                                     pos_offset=N_META, emit_u=False)
        b = _attention(q, k, vt, (km, vtm), batch=batch, seq=seq, tile=tile, tiles=ATTN_TILES)
        h = _post_attention(h, a, b, lw, i, g_final, tile=tile, final_norm=last)
    return h.reshape(batch, seq, D_MODEL)
```

```python
import functools
import math

import jax
import jax.numpy as jnp
import numpy as np
from jax import lax
from jax.experimental import pallas as pl
from jax.experimental.pallas import tpu as pltpu

D_MODEL = 1024
N_META = 16
POOL_WINDOWS = (2, 4, 8, 16)
POOL_GROUP = 128
POOL_WIDTH = POOL_GROUP * len(POOL_WINDOWS)
N_HEADS = 16
QK_NOPE = 64
QK_ROPE = 32
V_DIM = 64
Q_RANK = 256
KV_RANK = 128
QK_DIM = QK_NOPE + QK_ROPE
ROPE_HALF = QK_ROPE // 2
ATT_WIDTH = N_HEADS * V_DIM
ROPE_THETA = 10000.0
D_FF = 2816
NORM_EPS = 1e-6
MASK_VALUE = -1e30

LANES = 128
HEAD_PAD = LANES
MAX_WINDOW = max(POOL_WINDOWS)
assert all(w & (w - 1) == 0 for w in POOL_WINDOWS)
POOL_PAD = 8
Q_SCALE = (QK_DIM ** -0.5) * math.log2(math.e)
META_TILE = 128
HEAD_GROUP = 2
ONES_ROWS = 16
POST_SPLIT = 2
PRE_SPLIT = 2
MXU_ROWS = 256
ATTN_TILES = 1
SCORE_PITCH_PAD = 128
DIAG_BLOCK = 256
VMEM_LIMIT = 56 * 1024 * 1024

_NT = (((1,), (1,)), ((), ()))


def _rmsnorm(x, g):
    return x * lax.rsqrt(jnp.mean(x * x, axis=-1, keepdims=True) + NORM_EPS) * g


def _dot(a, b):
    return jnp.dot(a, b, preferred_element_type=jnp.float32)


def _const_spec(shape, layer=None):
    zeros = (0,) * len(shape)
    if layer is None:
        return pl.BlockSpec(shape, lambda *_: zeros, pipeline_mode=pl.Buffered(1))
    return pl.BlockSpec((None,) + tuple(shape), lambda *_: (layer,) + zeros,
                        pipeline_mode=pl.Buffered(1))


def _pre_attention_body(h_ref, gmix_ref, win_ref, poolw_ref, pscale_ref, qg_ref, kvg_ref,
                        wqt_ref, wk_ref, wvt_ref, cosq_ref, sinq_ref, cosk_ref, sink_ref, halo_ref,
                        q_out, k_out, vt_out, a_out, *rest, tile, pos_offset, emit_u):
    if emit_u:
        u_out, pool_scr = rest
    else:
        (pool_scr,) = rest
    j = pl.program_id(1)
    row0 = pl.multiple_of(j * tile, tile)
    first, end = POOL_PAD + MAX_WINDOW, POOL_PAD + MAX_WINDOW + tile

    @pl.when(j == 0)
    def _():
        pool_scr[:, 0:POOL_PAD, :] = jnp.zeros((3, POOL_PAD, POOL_WIDTH), jnp.float32)
        pool_scr[0, POOL_PAD:first, :] = halo_ref[...]

    step = min(tile, max(tile // PRE_SPLIT, MXU_ROWS))
    parts = [r * step for r in range(tile // step)]
    hn = [_rmsnorm(h_ref[lo:lo + step, :], gmix_ref[...]).astype(jnp.bfloat16) for lo in parts]
    zs = [_dot(x, win_ref[:, POOL_WIDTH:]) for x in hn]
    us = [_dot(x, win_ref[:, 0:POOL_WIDTH]) for x in hn]

    cqn = [_rmsnorm(z[:, 0:Q_RANK], qg_ref[...]).astype(jnp.bfloat16) for z in zs]
    qts = [lax.dot_general(wqt_ref[...], c, _NT, preferred_element_type=jnp.float32)
           for c in cqn]
    for lo, qt_raw in zip(parts, qts):
        cos_q, sin_q = cosq_ref[j, :, lo:lo + step], sinq_ref[j, :, lo:lo + step]
        for hd in range(N_HEADS):
            x = qt_raw[hd * HEAD_PAD:(hd + 1) * HEAD_PAD, :]
            partner = jnp.concatenate([x[ROPE_HALF:QK_ROPE], x[:ROPE_HALF], x[QK_ROPE:]], axis=0)
            q_out[hd * HEAD_PAD:(hd + 1) * HEAD_PAD, lo:lo + step] = (
                x * cos_q + partner * sin_q).astype(jnp.bfloat16)

    ckvn = [_rmsnorm(z[:, Q_RANK:Q_RANK + KV_RANK], kvg_ref[...]).astype(jnp.bfloat16) for z in zs]
    k_raws = [_dot(c, wk_ref[...]) for c in ckvn]
    for lo, z, k_raw in zip(parts, zs, k_raws):
        kr = z[:, Q_RANK + KV_RANK:]
        lane = lax.broadcasted_iota(jnp.int32, kr.shape, 1)
        kr_partner = jnp.where(lane < ROPE_HALF, pltpu.roll(kr, LANES - ROPE_HALF, 1),
                               pltpu.roll(kr, ROPE_HALF, 1))
        seq_rows = pl.ds(pl.multiple_of(row0 + lo, step), step)
        kr_roped = kr * cosk_ref[seq_rows, :] + kr_partner * sink_ref[seq_rows, :]
        for hd in range(N_HEADS):
            k_out[hd, lo:lo + step, :] = (
                k_raw[:, hd * HEAD_PAD:(hd + 1) * HEAD_PAD] + kr_roped).astype(jnp.bfloat16)
    for lo, c in zip(parts, ckvn):
        vt = lax.dot_general(wvt_ref[...], c, _NT, preferred_element_type=jnp.float32)
        vt_out[0, :, lo:lo + step] = vt.astype(jnp.bfloat16)

    for lo, u in zip(parts, us):
        pool_scr[0, first + lo:first + lo + step, :] = u
        if emit_u:
            u_out[lo:lo + step, :] = u
    g1, g2, g3 = POOL_GROUP, 2 * POOL_GROUP, 3 * POOL_GROUP
    pool_scr[1, POOL_PAD:end, :] = pool_scr[0, POOL_PAD:end, :] + pool_scr[0, POOL_PAD - 1:end - 1, :]
    pool_scr[2, POOL_PAD:end, g1:] = (pool_scr[1, POOL_PAD:end, g1:]
                                      + pool_scr[1, POOL_PAD - 2:end - 2, g1:])
    pool_scr[1, POOL_PAD:end, g2:] = (pool_scr[2, POOL_PAD:end, g2:]
                                      + pool_scr[2, POOL_PAD - 4:end - 4, g2:])
    wsums = [pool_scr[1, first:end, 0:g1], pool_scr[2, first:end, g1:g2],
             pool_scr[1, first:end, g2:g3],
             pool_scr[1, first:end, g3:] + pool_scr[1, first - 8:end - 8, g3:]]
    outs = []
    for g, w in enumerate(POOL_WINDOWS):
        if pos_offset + 1 >= MAX_WINDOW:
            mean = wsums[g] * (1.0 / w)
        else:
            pos = pos_offset + row0 + lax.broadcasted_iota(jnp.int32, (tile, POOL_GROUP), 0)
            mean = wsums[g] / jnp.minimum(pos + 1, w).astype(jnp.float32)
        y = (mean - pool_scr[0, first:end, g * POOL_GROUP:(g + 1) * POOL_GROUP]).astype(jnp.bfloat16)
        outs.append(_dot(y, poolw_ref[g]))
    a = jnp.concatenate(outs, axis=1) * pscale_ref[...]
    a_out[...] = a.astype(jnp.bfloat16)
    pool_scr[0, POOL_PAD:first, :] = pool_scr[0, end - MAX_WINDOW:end, :]


def _pre_attention(h, lw, layer, tables, halo, *, batch, seq, tile, pos_offset, emit_u):
    rows = batch * seq
    n_tiles = seq // tile
    row_map = lambda b, j: (b * n_tiles + j, 0)
    in_specs = [
        pl.BlockSpec((tile, D_MODEL), row_map),
        _const_spec((1, D_MODEL), layer),
        _const_spec((D_MODEL, 1024), layer),
        _const_spec((len(POOL_WINDOWS), POOL_GROUP, POOL_GROUP), layer),
        _const_spec((1, POOL_WIDTH), layer),
        _const_spec((1, Q_RANK), layer),
        _const_spec((1, KV_RANK), layer),
        _const_spec((N_HEADS * HEAD_PAD, Q_RANK), layer),
        _const_spec((KV_RANK, N_HEADS * HEAD_PAD), layer),
        _const_spec((ATT_WIDTH, KV_RANK), layer),
        _const_spec((n_tiles, HEAD_PAD, tile)),
        _const_spec((n_tiles, HEAD_PAD, tile)),
        _const_spec((seq, LANES)),
        _const_spec((seq, LANES)),
        pl.BlockSpec((MAX_WINDOW, POOL_WIDTH), lambda b, j: (0, 0)),
    ]
    out_shape = [
        jax.ShapeDtypeStruct((N_HEADS * HEAD_PAD, rows), jnp.bfloat16),
        jax.ShapeDtypeStruct((N_HEADS, rows, HEAD_PAD), jnp.bfloat16),
        jax.ShapeDtypeStruct((rows // tile, ATT_WIDTH, tile), jnp.bfloat16),
        jax.ShapeDtypeStruct((rows, POOL_WIDTH), jnp.bfloat16),
    ]
    out_specs = [
        pl.BlockSpec((N_HEADS * HEAD_PAD, tile), lambda b, j: (0, b * n_tiles + j)),
        pl.BlockSpec((N_HEADS, tile, HEAD_PAD), lambda b, j: (0, b * n_tiles + j, 0)),
        pl.BlockSpec((1, ATT_WIDTH, tile), lambda b, j: (b * n_tiles + j, 0, 0)),
        pl.BlockSpec((tile, POOL_WIDTH), row_map),
    ]
    if emit_u:
        out_shape.append(jax.ShapeDtypeStruct((rows, POOL_WIDTH), jnp.float32))
        out_specs.append(pl.BlockSpec((tile, POOL_WIDTH), row_map))
    body = functools.partial(_pre_attention_body, tile=tile, pos_offset=pos_offset, emit_u=emit_u)
    return pl.pallas_call(
        body,
        grid=(batch, n_tiles),
        in_specs=in_specs,
        out_specs=out_specs,
        out_shape=out_shape,
        scratch_shapes=[pltpu.VMEM((3, POOL_PAD + MAX_WINDOW + tile, POOL_WIDTH), jnp.float32)],
        compiler_params=pltpu.CompilerParams(
            dimension_semantics=("arbitrary", "arbitrary"), vmem_limit_bytes=VMEM_LIMIT),
        name="pre_attention",
    )(h, lw["g_mix"], lw["w_in_a"], lw["pool_w"], lw["pool_scale"], lw["q_g"], lw["kv_g"],
      lw["w_qt"], lw["w_k"], lw["w_vt"], *tables, halo)


def _attention_body(*refs, tile, tiles, diag, has_meta, group):
    if has_meta:
        q_ref, k_ref, vt_ref, tri_ref, km_ref, vtm_ref, o_ref, s_scr, smax_scr, m_scr, acc_scr = refs
    else:
        q_ref, k_ref, vt_ref, tri_ref, o_ref, s_scr, smax_scr, m_scr, acc_scr = refs
    step = pl.program_id(1)
    n_groups = N_HEADS // group
    n_sub = tile // diag

    def q_block(qt, hd, lo=0):
        return q_ref[hd * HEAD_PAD:(hd + 1) * HEAD_PAD, qt * tile + lo:(qt + 1) * tile]

    def produce_full(slot, qt, hg, t):
        k_rows = pl.ds(pl.multiple_of(t * tile, tile), tile)
        for g in range(group):
            hd = hg * group + g
            s = _dot(k_ref[hd, k_rows, :], q_block(qt, hd))
            s_scr[slot, g, 0:tile, 0:tile] = s
            smax_scr[slot, g] = jnp.max(s, axis=0, keepdims=True)

    def produce_diag(slot, qt, hg, t):
        for g in range(group):
            hd = hg * group + g
            smax = None
            for c in range(n_sub):
                lo = c * diag
                k = k_ref[hd, pl.ds(pl.multiple_of(t * tile + lo, diag), diag), :]
                if c == 0 and has_meta:
                    k = jnp.concatenate([k, km_ref[hd, 0:N_META, :]], axis=0)
                s = _dot(k, q_block(qt, hd, lo))
                on_diag = s[0:diag, 0:diag] + tri_ref[...]
                main = on_diag if lo + diag == tile else jnp.concatenate(
                    [on_diag, s[0:diag, diag:]], axis=1)
                s_scr[slot, g, lo:lo + diag, lo:tile] = main
                cmax = jnp.max(main, axis=0, keepdims=True)
                if c == 0 and has_meta:
                    s_scr[slot, g, tile:tile + N_META, 0:tile] = s[diag:, :]
                    cmax = jnp.maximum(cmax, jnp.max(s[diag:, :], axis=0, keepdims=True))
                smax = cmax if c == 0 else jnp.concatenate(
                    [smax[:, :lo], jnp.maximum(smax[:, lo:], cmax)], axis=1)
            smax_scr[slot, g] = smax

    def consume(slot, qt, hg, t, is_diag):
        for g in range(group):
            hd = hg * group + g
            state = qt * N_HEADS + hd
            v_rows = pl.ds(pl.multiple_of(hd * V_DIM, V_DIM), V_DIM)
            m_old = m_scr[state]
            m_new = jnp.maximum(m_old, smax_scr[slot, g])
            if not is_diag:
                p = jnp.exp2(s_scr[slot, g, 0:tile, 0:tile] - m_new).astype(jnp.bfloat16)
                ones = jnp.ones((ONES_ROWS, tile), jnp.bfloat16)
                pv = _dot(jnp.concatenate([vt_ref[t, v_rows, :], ones], axis=0), p)
            else:
                ones = jnp.ones((ONES_ROWS, diag), jnp.bfloat16)
                for c in range(n_sub):
                    lo = c * diag
                    p = jnp.exp2(s_scr[slot, g, lo:lo + diag, lo:tile] - m_new[:, lo:]).astype(jnp.bfloat16)
                    part = _dot(jnp.concatenate([vt_ref[t, v_rows, lo:lo + diag], ones], axis=0), p)
                    pv = part if c == 0 else jnp.concatenate(
                        [pv[:, :lo], pv[:, lo:] + part], axis=1)
                if has_meta:
                    pm = jnp.exp2(s_scr[slot, g, tile:tile + N_META, 0:tile] - m_new).astype(jnp.bfloat16)
                    ones_meta = jnp.ones((ONES_ROWS, N_META), jnp.bfloat16)
                    pv = pv + _dot(
                        jnp.concatenate([vtm_ref[0, v_rows, 0:N_META], ones_meta], axis=0), pm)
            acc_scr[state] = jnp.exp2(m_old - m_new) * acc_scr[state] + pv
            m_scr[state] = m_new

    def finalize_pair(qt, pair):
        acc0, acc1 = acc_scr[qt * N_HEADS + 2 * pair], acc_scr[qt * N_HEADS + 2 * pair + 1]
        both = jnp.concatenate([acc0[0:V_DIM] / acc0[V_DIM:V_DIM + 1],
                                acc1[0:V_DIM] / acc1[V_DIM:V_DIM + 1]], axis=0)
        o_ref[qt * tile:(qt + 1) * tile, pair * LANES:(pair + 1) * LANES] = both.T.astype(jnp.bfloat16)

    def key_block(qt, t, is_diag, produce_next):
        for hg in range(n_groups):
            if hg + 1 < n_groups:
                (produce_diag if is_diag else produce_full)((hg + 1) % 2, qt, hg + 1, t)
            elif produce_next is not None:
                produce_next()
            consume(hg % 2, qt, hg, t, is_diag)
            if is_diag:
                for hd in range(hg * group, (hg + 1) * group, 2):
                    finalize_pair(qt, hd // 2)

    m_scr[...] = jnp.full_like(m_scr, MASK_VALUE)
    acc_scr[...] = jnp.zeros_like(acc_scr)

    for qt in range(tiles):
        i = step * tiles + qt
        first_of_next = (lambda qt=qt: produce_full(0, qt + 1, 0, 0)) if qt + 1 < tiles else None

        def full_blocks_then_diag(qt=qt, i=i, first_of_next=first_of_next):
            def full_block(t, carry):
                key_block(qt, t, False, lambda: produce_full(0, qt, 0, t + 1))
                return carry
            lax.fori_loop(0, i - 1, full_block, 0)
            key_block(qt, i - 1, False, lambda: produce_diag(0, qt, 0, i))
            key_block(qt, i, True, first_of_next)

        if qt > 0:
            full_blocks_then_diag()
        else:
            @pl.when(i == 0)
            def _(qt=qt, i=i, first_of_next=first_of_next):
                produce_diag(0, qt, 0, i)
                key_block(qt, i, True, first_of_next)

            @pl.when(i > 0)
            def _(qt=qt, run=full_blocks_then_diag):
                produce_full(0, qt, 0, 0)
                run()


def _attention(q, k, vt, meta_kv, *, batch, seq, tile, tiles):
    rows = batch * seq
    nq = seq // tile
    steps = nq // tiles
    has_meta = meta_kv is not None
    assert (N_HEADS // HEAD_GROUP) % 2 == 0 and HEAD_GROUP % 2 == 0 and nq % tiles == 0
    diag = min(tile, DIAG_BLOCK)
    idx = np.arange(diag)
    tri = np.where(idx[:, None] <= idx[None, :], 0.0, MASK_VALUE).astype(np.float32)
    in_specs = [
        pl.BlockSpec((N_HEADS * HEAD_PAD, tiles * tile), lambda b, i: (0, b * steps + i)),
        pl.BlockSpec((N_HEADS, seq, HEAD_PAD), lambda b, i: (0, b, 0)),
        pl.BlockSpec((nq, ATT_WIDTH, tile), lambda b, i: (b, 0, 0)),
        _const_spec((diag, diag)),
    ]
    args = [q, k, vt, tri]
    if has_meta:
        km, vtm = meta_kv
        in_specs += [_const_spec(km.shape), _const_spec(vtm.shape)]
        args += [km, vtm]
    body = functools.partial(_attention_body, tile=tile, tiles=tiles, diag=diag, has_meta=has_meta,
                             group=HEAD_GROUP)
    return pl.pallas_call(
        body,
        grid=(batch, steps),
        in_specs=in_specs,
        out_specs=pl.BlockSpec((tiles * tile, ATT_WIDTH), lambda b, i: (b * steps + i, 0)),
        out_shape=jax.ShapeDtypeStruct((rows, ATT_WIDTH), jnp.bfloat16),
        scratch_shapes=[
            pltpu.VMEM((2, HEAD_GROUP, tile + N_META, tile + SCORE_PITCH_PAD), jnp.float32),
            pltpu.VMEM((2, HEAD_GROUP, 1, tile), jnp.float32),
            pltpu.VMEM((tiles * N_HEADS, 1, tile), jnp.float32),
            pltpu.VMEM((tiles * N_HEADS, V_DIM + ONES_ROWS, tile), jnp.float32),
        ],
        compiler_params=pltpu.CompilerParams(
            dimension_semantics=("parallel", "parallel"), vmem_limit_bytes=VMEM_LIMIT),
        name="attention",
    )(*args)


def _post_attention_body(h_ref, a_ref, b_ref, gmix_ref, wgates_ref, wpa_ref, wpb_ref, wo_ref,
                         gffn_ref, wgate_ref, wup_ref, wdown_ref, gfin_ref, o_ref, *, final_norm):
    tile = h_ref.shape[0]
    step = min(tile, max(tile // POST_SPLIT, MXU_ROWS))
    parts = [pl.ds(r * step, step) for r in range(tile // step)]
    pa = [_dot(a_ref[r, :], wpa_ref[...]) for r in parts]
    pb = [_dot(b_ref[r, :], wpb_ref[...]) for r in parts]
    hs = [h_ref[r, :] for r in parts]
    hn = [_rmsnorm(h, gmix_ref[...]).astype(jnp.bfloat16) for h in hs]
    gates = [jax.nn.sigmoid(_dot(x, wgates_ref[...])) for x in hn]
    merged = [(g[:, :D_MODEL] * xa + g[:, D_MODEL:] * xb).astype(jnp.bfloat16)
              for g, xa, xb in zip(gates, pa, pb)]
    hs = [h + _dot(m, wo_ref[...]) for h, m in zip(hs, merged)]
    hn = [_rmsnorm(h, gffn_ref[...]).astype(jnp.bfloat16) for h in hs]
    act = [(jax.nn.silu(_dot(x, wgate_ref[...])) * _dot(x, wup_ref[...])).astype(jnp.bfloat16)
           for x in hn]
    hs = [h + _dot(x, wdown_ref[...]) for h, x in zip(hs, act)]
    for r, h in zip(parts, hs):
        o_ref[r, :] = _rmsnorm(h, gfin_ref[...]) if final_norm else h


def _post_attention(h, a, b, lw, layer, g_final, *, tile, final_norm):
    rows = h.shape[0]
    row_map = lambda r: (r, 0)
    body = functools.partial(_post_attention_body, final_norm=final_norm)
    return pl.pallas_call(
        body,
        grid=(rows // tile,),
        in_specs=[
            pl.BlockSpec((tile, D_MODEL), row_map),
            pl.BlockSpec((tile, POOL_WIDTH), row_map),
            pl.BlockSpec((tile, ATT_WIDTH), row_map),
            _const_spec((1, D_MODEL), layer),
            _const_spec((D_MODEL, 2 * D_MODEL), layer),
            _const_spec((POOL_WIDTH, D_MODEL), layer),
            _const_spec((ATT_WIDTH, D_MODEL), layer),
            _const_spec((D_MODEL, D_MODEL), layer),
            _const_spec((1, D_MODEL), layer),
            _const_spec((D_MODEL, D_FF), layer),
            _const_spec((D_MODEL, D_FF), layer),
            _const_spec((D_FF, D_MODEL), layer),
            _const_spec((1, D_MODEL)),
        ],
        out_specs=pl.BlockSpec((tile, D_MODEL), row_map),
        out_shape=jax.ShapeDtypeStruct((rows, D_MODEL), jnp.float32),
        compiler_params=pltpu.CompilerParams(
            dimension_semantics=("parallel",), vmem_limit_bytes=VMEM_LIMIT),
        name="post_attention",
    )(h, a, b, lw["g_mix"], lw["w_in_gates"], lw["w_pa"], lw["w_pb"], lw["w_o"], lw["g_ffn"],
      lw["w_gate"], lw["w_up"], lw["w_down"], g_final)


def _head_lanes():
    return [(QK_NOPE, QK_NOPE + ROPE_HALF), (QK_NOPE + ROPE_HALF, QK_DIM), (0, QK_NOPE),
            (None, HEAD_PAD - QK_DIM)]


def _place_head_columns(w, keep, xp=jnp):
    parts = []
    for start, stop in _head_lanes():
        if start is None or not keep(start):
            width = stop if start is None else stop - start
            parts.append(xp.zeros(w.shape[:-1] + (width,), w.dtype))
        else:
            parts.append(w[..., start:stop])
    return xp.concatenate(parts, axis=-1)


def _prepare_weights(norm_mix_g, w_in, pool_w, pool_scale, q_norm_g, kv_norm_g, w_uq, w_ukv,
                     w_pa, w_pb, w_o, norm_ffn_g, w_gate, w_up, w_down):
    bf = jnp.bfloat16
    depth = w_in.shape[0]
    o_q = POOL_WIDTH + Q_RANK
    o_kv = o_q + KV_RANK
    o_kr = o_kv + QK_ROPE
    is_nope = lambda col: col < QK_NOPE
    kr_cols = _place_head_columns(jnp.pad(w_in[..., o_kv:o_kr], ((0, 0), (0, 0), (QK_NOPE, 0))),
                                  lambda col: not is_nope(col))
    w_in_a = jnp.concatenate([w_in[..., :o_kv], kr_cols], axis=-1).astype(bf)
    w_q = _place_head_columns(w_uq.reshape(depth, Q_RANK, N_HEADS, QK_DIM), lambda col: True)
    ukv = w_ukv.reshape(depth, KV_RANK, N_HEADS, QK_NOPE + V_DIM)
    w_k = _place_head_columns(jnp.pad(ukv[..., :QK_NOPE], [(0, 0)] * 3 + [(0, QK_ROPE)]), is_nope)
    w_vt = jnp.swapaxes(ukv[..., QK_NOPE:].reshape(depth, KV_RANK, ATT_WIDTH), 1, 2)
    row = lambda g: g[:, None, :]
    return dict(
        g_mix=row(norm_mix_g), w_in_a=w_in_a, w_in_gates=w_in[..., o_kr:].astype(bf),
        pool_w=pool_w.astype(bf), pool_scale=row(pool_scale),
        q_g=row(q_norm_g), kv_g=row(kv_norm_g),
        w_qt=jnp.swapaxes(w_q.reshape(depth, Q_RANK, N_HEADS * HEAD_PAD), 1, 2).astype(bf),
        w_k=w_k.reshape(depth, KV_RANK, N_HEADS * HEAD_PAD).astype(bf), w_vt=w_vt.astype(bf),
        w_pa=w_pa.astype(bf), w_pb=w_pb.astype(bf), w_o=w_o.astype(bf),
        g_ffn=row(norm_ffn_g), w_gate=w_gate.astype(bf), w_up=w_up.astype(bf),
        w_down=w_down.astype(bf))


def _rope_tables(length):
    f32 = np.float32
    inv = f32(1.0) / (f32(ROPE_THETA) ** (np.arange(0, QK_ROPE, 2, dtype=f32) / f32(QK_ROPE)))
    ang = np.arange(length, dtype=f32)[:, None] * inv[None, :]
    cos, sin = np.cos(ang).astype(f32), np.sin(ang).astype(f32)
    ones = np.ones((length, QK_NOPE), f32)
    cos_t = np.asarray(_place_head_columns(np.concatenate([ones, cos, cos], axis=1),
                                           lambda col: True, xp=np))
    sin_t = np.asarray(_place_head_columns(np.concatenate([0 * ones, -sin, sin], axis=1),
                                           lambda col: True, xp=np))
    return cos_t * f32(Q_SCALE), sin_t * f32(Q_SCALE), cos_t, sin_t


def _tile_q_tables(tables, tile):
    cos_q, sin_q, cos_k, sin_k = tables
    per_tile = lambda t: np.ascontiguousarray(t.reshape(-1, tile, HEAD_PAD).transpose(0, 2, 1))
    return per_tile(cos_q), per_tile(sin_q), cos_k, sin_k


def kernel(x, meta_tokens, norm_mix_g, w_in, pool_w, pool_scale, q_norm_g, kv_norm_g, w_uq, w_ukv,
           w_pa, w_pb, w_o, norm_ffn_g, w_gate, w_up, w_down, final_norm_g):
    batch, seq, _ = x.shape
    depth = w_in.shape[0]
    tile = 512
    tables = _rope_tables(N_META + max(seq, META_TILE))
    meta_tables = _tile_q_tables(tuple(t[:META_TILE] for t in tables), META_TILE)
    main_tables = _tile_q_tables(tuple(t[N_META:N_META + seq] for t in tables), tile)
    g_final = final_norm_g[None]
    lw = _prepare_weights(norm_mix_g, w_in, pool_w, pool_scale, q_norm_g, kv_norm_g, w_uq, w_ukv,
                          w_pa, w_pb, w_o, norm_ffn_g, w_gate, w_up, w_down)

    h = x.reshape(batch * seq, D_MODEL)
    hm = jnp.pad(meta_tokens.astype(x.dtype), ((0, META_TILE - N_META), (0, 0)))
    zero_halo = np.zeros((MAX_WINDOW, POOL_WIDTH), np.float32)
    for i in range(depth):
        last = i == depth - 1
        qm, km, vtm, am, um = _pre_attention(hm, lw, i, meta_tables, zero_halo, batch=1,
                                             seq=META_TILE, tile=META_TILE, pos_offset=0,
                                             emit_u=True)
        if not last:
            bm = _attention(qm, km, vtm, None, batch=1, seq=META_TILE, tile=META_TILE, tiles=1)
            hm = _post_attention(hm, am, bm, lw, i, g_final, tile=META_TILE, final_norm=False)
        q, k, vt, a = _pre_attention(h, lw, i, main_tables, um, batch=batch, seq=seq, tile=tile,
                                     pos_offset=N_META, emit_u=False)
        b = _attention(q, k, vt, (km, vtm), batch=batch, seq=seq, tile=tile, tiles=ATTN_TILES)
        h = _post_attention(h, a, b, lw, i, g_final, tile=tile, final_norm=last)
    return h.reshape(batch, seq, D_MODEL)
```

```python
import functools
import math

import jax
import jax.numpy as jnp
import numpy as np
from jax import lax
from jax.experimental import pallas as pl
from jax.experimental.pallas import tpu as pltpu

D_MODEL = 1024
N_META = 16
POOL_WINDOWS = (2, 4, 8, 16)
POOL_GROUP = 128
POOL_WIDTH = POOL_GROUP * len(POOL_WINDOWS)
N_HEADS = 16
QK_NOPE = 64
QK_ROPE = 32
V_DIM = 64
Q_RANK = 256
KV_RANK = 128
QK_DIM = QK_NOPE + QK_ROPE
ROPE_HALF = QK_ROPE // 2
ATT_WIDTH = N_HEADS * V_DIM
ROPE_THETA = 10000.0
D_FF = 2816
NORM_EPS = 1e-6
MASK_VALUE = -1e30

LANES = 128
HEAD_PAD = LANES
W_IN_A_WIDTH = POOL_WIDTH + Q_RANK + KV_RANK + HEAD_PAD
MAX_WINDOW = max(POOL_WINDOWS)
assert all(w & (w - 1) == 0 for w in POOL_WINDOWS)
POOL_PAD = 8
Q_SCALE = (QK_DIM ** -0.5) * math.log2(math.e)
ROW_TILE = 512
META_TILE = 128
HEAD_GROUP = 2
ONES_ROWS = 16
POST_SPLIT = 2
PRE_SPLIT = 2
Q_HEAD_CHUNK = 4
MXU_ROWS = 256
ATTN_TILES = 1
DIAG_BLOCK = 256
V7X_VMEM_BYTES = 64 * 1024 * 1024
VMEM_LIMIT = V7X_VMEM_BYTES * 7 // 8

_NT = (((1,), (1,)), ((), ()))


def _rmsnorm(x, g):
    return x * lax.rsqrt(jnp.mean(x * x, axis=-1, keepdims=True) + NORM_EPS) * g


def _dot(a, b):
    return jnp.dot(a, b, preferred_element_type=jnp.float32)


def _const_spec(shape, layer=None):
    zeros = (0,) * len(shape)
    if layer is None:
        return pl.BlockSpec(shape, lambda *_: zeros, pipeline_mode=pl.Buffered(1))
    return pl.BlockSpec((None,) + tuple(shape), lambda *_: (layer,) + zeros,
                        pipeline_mode=pl.Buffered(1))


def _pre_attention_body(h_ref, gmix_ref, win_ref, poolw_ref, pscale_ref, qg_ref, kvg_ref,
                        wqt_ref, wk_ref, wvt_ref, cosq_ref, sinq_ref, cosk_ref, sink_ref, halo_ref,
                        q_out, k_out, vt_out, a_out, *rest, tile, pos_offset, emit_u):
    if emit_u:
        u_out, pool_scr = rest
    else:
        (pool_scr,) = rest
    j = pl.program_id(1)
    row0 = pl.multiple_of(j * tile, tile)
    first, end = POOL_PAD + MAX_WINDOW, POOL_PAD + MAX_WINDOW + tile

    @pl.when(j == 0)
    def _():
        pool_scr[:, 0:POOL_PAD, :] = jnp.zeros((3, POOL_PAD, POOL_WIDTH), jnp.float32)
        pool_scr[0, POOL_PAD:first, :] = halo_ref[...]

    step = min(tile, max(tile // PRE_SPLIT, MXU_ROWS))
    parts = [r * step for r in range(tile // step)]
    hn = [_rmsnorm(h_ref[lo:lo + step, :], gmix_ref[...]).astype(jnp.bfloat16) for lo in parts]
    zs = [_dot(x, win_ref[:, POOL_WIDTH:]) for x in hn]
    us = [_dot(x, win_ref[:, 0:POOL_WIDTH]) for x in hn]

    cqn = [_rmsnorm(z[:, 0:Q_RANK], qg_ref[...]).astype(jnp.bfloat16) for z in zs]
    for lo, c in zip(parts, cqn):
        cos_q, sin_q = cosq_ref[j, :, lo:lo + step], sinq_ref[j, :, lo:lo + step]
        for h0 in range(0, N_HEADS, Q_HEAD_CHUNK):
            head_rows = slice(h0 * HEAD_PAD, (h0 + Q_HEAD_CHUNK) * HEAD_PAD)
            qt_raw = lax.dot_general(wqt_ref[head_rows, :], c, _NT,
                                     preferred_element_type=jnp.float32)
            for hd in range(Q_HEAD_CHUNK):
                x = qt_raw[hd * HEAD_PAD:(hd + 1) * HEAD_PAD, :]
                partner = jnp.concatenate([x[ROPE_HALF:QK_ROPE], x[:ROPE_HALF], x[QK_ROPE:]], axis=0)
                q_out[(h0 + hd) * HEAD_PAD:(h0 + hd + 1) * HEAD_PAD, lo:lo + step] = (
                    x * cos_q + partner * sin_q).astype(jnp.bfloat16)

    ckvn = [_rmsnorm(z[:, Q_RANK:Q_RANK + KV_RANK], kvg_ref[...]).astype(jnp.bfloat16) for z in zs]
    k_raws = [_dot(c, wk_ref[...]) for c in ckvn]
    for lo, z, k_raw in zip(parts, zs, k_raws):
        kr = z[:, Q_RANK + KV_RANK:]
        lane = lax.broadcasted_iota(jnp.int32, kr.shape, 1)
        kr_partner = jnp.where(lane < ROPE_HALF, pltpu.roll(kr, LANES - ROPE_HALF, 1),
                               pltpu.roll(kr, ROPE_HALF, 1))
        seq_rows = pl.ds(pl.multiple_of(row0 + lo, step), step)
        kr_roped = kr * cosk_ref[seq_rows, :] + kr_partner * sink_ref[seq_rows, :]
        for hd in range(N_HEADS):
            k_out[hd, lo:lo + step, :] = (
                k_raw[:, hd * HEAD_PAD:(hd + 1) * HEAD_PAD] + kr_roped).astype(jnp.bfloat16)
    for lo, c in zip(parts, ckvn):
        vt = lax.dot_general(wvt_ref[...], c, _NT, preferred_element_type=jnp.float32)
        vt_out[0, :, lo:lo + step] = vt.astype(jnp.bfloat16)

    for lo, u in zip(parts, us):
        pool_scr[0, first + lo:first + lo + step, :] = u
        if emit_u:
            u_out[lo:lo + step, :] = u
    g1, g2, g3 = POOL_GROUP, 2 * POOL_GROUP, 3 * POOL_GROUP
    pool_scr[1, POOL_PAD:end, :] = pool_scr[0, POOL_PAD:end, :] + pool_scr[0, POOL_PAD - 1:end - 1, :]
    pool_scr[2, POOL_PAD:end, g1:] = (pool_scr[1, POOL_PAD:end, g1:]
                                      + pool_scr[1, POOL_PAD - 2:end - 2, g1:])
    pool_scr[1, POOL_PAD:end, g2:] = (pool_scr[2, POOL_PAD:end, g2:]
                                      + pool_scr[2, POOL_PAD - 4:end - 4, g2:])
    wsums = [pool_scr[1, first:end, 0:g1], pool_scr[2, first:end, g1:g2],
             pool_scr[1, first:end, g2:g3],
             pool_scr[1, first:end, g3:] + pool_scr[1, first - 8:end - 8, g3:]]
    outs = []
    for g, w in enumerate(POOL_WINDOWS):
        if pos_offset + 1 >= MAX_WINDOW:
            mean = wsums[g] * (1.0 / w)
        else:
            pos = pos_offset + row0 + lax.broadcasted_iota(jnp.int32, (tile, POOL_GROUP), 0)
            mean = wsums[g] / jnp.minimum(pos + 1, w).astype(jnp.float32)
        y = (mean - pool_scr[0, first:end, g * POOL_GROUP:(g + 1) * POOL_GROUP]).astype(jnp.bfloat16)
        outs.append(_dot(y, poolw_ref[g]))
    a = jnp.concatenate(outs, axis=1) * pscale_ref[...]
    a_out[...] = a.astype(jnp.bfloat16)
    pool_scr[0, POOL_PAD:first, :] = pool_scr[0, end - MAX_WINDOW:end, :]


def _pre_attention(h, lw, layer, tables, halo, *, batch, seq, tile, pos_offset, emit_u):
    rows = batch * seq
    n_tiles = seq // tile
    row_map = lambda b, j: (b * n_tiles + j, 0)
    in_specs = [
        pl.BlockSpec((tile, D_MODEL), row_map),
        _const_spec((1, D_MODEL), layer),
        _const_spec((D_MODEL, W_IN_A_WIDTH), layer),
        _const_spec((len(POOL_WINDOWS), POOL_GROUP, POOL_GROUP), layer),
        _const_spec((1, POOL_WIDTH), layer),
        _const_spec((1, Q_RANK), layer),
        _const_spec((1, KV_RANK), layer),
        _const_spec((N_HEADS * HEAD_PAD, Q_RANK), layer),
        _const_spec((KV_RANK, N_HEADS * HEAD_PAD), layer),
        _const_spec((ATT_WIDTH, KV_RANK), layer),
        _const_spec((n_tiles, HEAD_PAD, tile)),
        _const_spec((n_tiles, HEAD_PAD, tile)),
        _const_spec((seq, LANES)),
        _const_spec((seq, LANES)),
        pl.BlockSpec((MAX_WINDOW, POOL_WIDTH), lambda b, j: (0, 0)),
    ]
    out_shape = [
        jax.ShapeDtypeStruct((N_HEADS * HEAD_PAD, rows), jnp.bfloat16),
        jax.ShapeDtypeStruct((N_HEADS, rows, HEAD_PAD), jnp.bfloat16),
        jax.ShapeDtypeStruct((rows // tile, ATT_WIDTH, tile), jnp.bfloat16),
        jax.ShapeDtypeStruct((rows, POOL_WIDTH), jnp.bfloat16),
    ]
    out_specs = [
        pl.BlockSpec((N_HEADS * HEAD_PAD, tile), lambda b, j: (0, b * n_tiles + j)),
        pl.BlockSpec((N_HEADS, tile, HEAD_PAD), lambda b, j: (0, b * n_tiles + j, 0)),
        pl.BlockSpec((1, ATT_WIDTH, tile), lambda b, j: (b * n_tiles + j, 0, 0)),
        pl.BlockSpec((tile, POOL_WIDTH), row_map),
    ]
    if emit_u:
        out_shape.append(jax.ShapeDtypeStruct((rows, POOL_WIDTH), jnp.float32))
        out_specs.append(pl.BlockSpec((tile, POOL_WIDTH), row_map))
    body = functools.partial(_pre_attention_body, tile=tile, pos_offset=pos_offset, emit_u=emit_u)
    return pl.pallas_call(
        body,
        grid=(batch, n_tiles),
        in_specs=in_specs,
        out_specs=out_specs,
        out_shape=out_shape,
        scratch_shapes=[pltpu.VMEM((3, POOL_PAD + MAX_WINDOW + tile, POOL_WIDTH), jnp.float32)],
        compiler_params=pltpu.CompilerParams(
            dimension_semantics=("arbitrary", "arbitrary"), vmem_limit_bytes=VMEM_LIMIT),
        name="pre_attention",
    )(h, lw["g_mix"], lw["w_in_a"], lw["pool_w"], lw["pool_scale"], lw["q_g"], lw["kv_g"],
      lw["w_qt"], lw["w_k"], lw["w_vt"], *tables, halo)


def _attention_body(*refs, tile, tiles, diag, has_meta, group):
    if has_meta:
        q_ref, k_ref, vt_ref, tri_ref, km_ref, vtm_ref, o_ref, s_scr, smax_scr, m_scr, acc_scr = refs
    else:
        q_ref, k_ref, vt_ref, tri_ref, o_ref, s_scr, smax_scr, m_scr, acc_scr = refs
    step = pl.program_id(1)
    n_groups = N_HEADS // group
    n_sub = tile // diag

    def q_block(qt, hd, lo=0):
        return q_ref[hd * HEAD_PAD:(hd + 1) * HEAD_PAD, qt * tile + lo:(qt + 1) * tile]

    def produce_full(slot, qt, hg, t):
        k_rows = pl.ds(pl.multiple_of(t * tile, tile), tile)
        for g in range(group):
            hd = hg * group + g
            s = _dot(k_ref[hd, k_rows, :], q_block(qt, hd))
            s_scr[slot, g, 0:tile, 0:tile] = s
            smax_scr[slot, g] = jnp.max(s, axis=0, keepdims=True)

    def produce_diag(slot, qt, hg, t):
        for g in range(group):
            hd = hg * group + g
            smax = None
            for c in range(n_sub):
                lo = c * diag
                k = k_ref[hd, pl.ds(pl.multiple_of(t * tile + lo, diag), diag), :]
                if c == 0 and has_meta:
                    k = jnp.concatenate([k, km_ref[hd, 0:N_META, :]], axis=0)
                s = _dot(k, q_block(qt, hd, lo))
                on_diag = s[0:diag, 0:diag] + tri_ref[...]
                main = on_diag if lo + diag == tile else jnp.concatenate(
                    [on_diag, s[0:diag, diag:]], axis=1)
                s_scr[slot, g, lo:lo + diag, lo:tile] = main
                cmax = jnp.max(main, axis=0, keepdims=True)
                if c == 0 and has_meta:
                    s_scr[slot, g, tile:tile + N_META, 0:tile] = s[diag:, :]
                    cmax = jnp.maximum(cmax, jnp.max(s[diag:, :], axis=0, keepdims=True))
                smax = cmax if c == 0 else jnp.concatenate(
                    [smax[:, :lo], jnp.maximum(smax[:, lo:], cmax)], axis=1)
            smax_scr[slot, g] = smax

    def consume(slot, qt, hg, t, is_diag):
        for g in range(group):
            hd = hg * group + g
            state = qt * N_HEADS + hd
            v_rows = pl.ds(pl.multiple_of(hd * V_DIM, V_DIM), V_DIM)
            m_old = m_scr[state]
            m_new = jnp.maximum(m_old, smax_scr[slot, g])
            if not is_diag:
                p = jnp.exp2(s_scr[slot, g, 0:tile, 0:tile] - m_new).astype(jnp.bfloat16)
                ones = jnp.ones((ONES_ROWS, tile), jnp.bfloat16)
                pv = _dot(jnp.concatenate([vt_ref[t, v_rows, :], ones], axis=0), p)
            else:
                ones = jnp.ones((ONES_ROWS, diag), jnp.bfloat16)
                for c in range(n_sub):
                    lo = c * diag
                    p = jnp.exp2(s_scr[slot, g, lo:lo + diag, lo:tile] - m_new[:, lo:]).astype(jnp.bfloat16)
                    part = _dot(jnp.concatenate([vt_ref[t, v_rows, lo:lo + diag], ones], axis=0), p)
                    pv = part if c == 0 else jnp.concatenate(
                        [pv[:, :lo], pv[:, lo:] + part], axis=1)
                if has_meta:
                    pm = jnp.exp2(s_scr[slot, g, tile:tile + N_META, 0:tile] - m_new).astype(jnp.bfloat16)
                    ones_meta = jnp.ones((ONES_ROWS, N_META), jnp.bfloat16)
                    pv = pv + _dot(
                        jnp.concatenate([vtm_ref[0, v_rows, 0:N_META], ones_meta], axis=0), pm)
            acc_scr[state] = jnp.exp2(m_old - m_new) * acc_scr[state] + pv
            m_scr[state] = m_new

    def finalize_pair(qt, pair):
        acc0, acc1 = acc_scr[qt * N_HEADS + 2 * pair], acc_scr[qt * N_HEADS + 2 * pair + 1]
        both = jnp.concatenate([acc0[0:V_DIM] / acc0[V_DIM:V_DIM + 1],
                                acc1[0:V_DIM] / acc1[V_DIM:V_DIM + 1]], axis=0)
        o_ref[qt * tile:(qt + 1) * tile, pair * LANES:(pair + 1) * LANES] = both.T.astype(jnp.bfloat16)

    def key_block(qt, t, is_diag, produce_next):
        for hg in range(n_groups):
            if hg + 1 < n_groups:
                (produce_diag if is_diag else produce_full)((hg + 1) % 2, qt, hg + 1, t)
            elif produce_next is not None:
                produce_next()
            consume(hg % 2, qt, hg, t, is_diag)
            if is_diag:
                for hd in range(hg * group, (hg + 1) * group, 2):
                    finalize_pair(qt, hd // 2)

    m_scr[...] = jnp.full_like(m_scr, MASK_VALUE)
    acc_scr[...] = jnp.zeros_like(acc_scr)

    for qt in range(tiles):
        i = step * tiles + qt
        first_of_next = (lambda qt=qt: produce_full(0, qt + 1, 0, 0)) if qt + 1 < tiles else None

        def full_blocks_then_diag(qt=qt, i=i, first_of_next=first_of_next):
            def full_block(t, carry):
                key_block(qt, t, False, lambda: produce_full(0, qt, 0, t + 1))
                return carry
            lax.fori_loop(0, i - 1, full_block, 0)
            key_block(qt, i - 1, False, lambda: produce_diag(0, qt, 0, i))
            key_block(qt, i, True, first_of_next)

        if qt > 0:
            full_blocks_then_diag()
        else:
            @pl.when(i == 0)
            def _(qt=qt, i=i, first_of_next=first_of_next):
                produce_diag(0, qt, 0, i)
                key_block(qt, i, True, first_of_next)

            @pl.when(i > 0)
            def _(qt=qt, run=full_blocks_then_diag):
                produce_full(0, qt, 0, 0)
                run()


def _attention(q, k, vt, meta_kv, *, batch, seq, tile, tiles):
    rows = batch * seq
    nq = seq // tile
    steps = nq // tiles
    has_meta = meta_kv is not None
    assert (N_HEADS // HEAD_GROUP) % 2 == 0 and HEAD_GROUP % 2 == 0 and nq % tiles == 0
    diag = min(tile, DIAG_BLOCK)
    idx = np.arange(diag)
    tri = np.where(idx[:, None] <= idx[None, :], 0.0, MASK_VALUE).astype(np.float32)
    in_specs = [
        pl.BlockSpec((N_HEADS * HEAD_PAD, tiles * tile), lambda b, i: (0, b * steps + i)),
        pl.BlockSpec((N_HEADS, seq, HEAD_PAD), lambda b, i: (0, b, 0)),
        pl.BlockSpec((nq, ATT_WIDTH, tile), lambda b, i: (b, 0, 0)),
        _const_spec((diag, diag)),
    ]
    args = [q, k, vt, tri]
    if has_meta:
        km, vtm = meta_kv
        in_specs += [_const_spec(km.shape), _const_spec(vtm.shape)]
        args += [km, vtm]
    body = functools.partial(_attention_body, tile=tile, tiles=tiles, diag=diag, has_meta=has_meta,
                             group=HEAD_GROUP)
    return pl.pallas_call(
        body,
        grid=(batch, steps),
        in_specs=in_specs,
        out_specs=pl.BlockSpec((tiles * tile, ATT_WIDTH), lambda b, i: (b * steps + i, 0)),
        out_shape=jax.ShapeDtypeStruct((rows, ATT_WIDTH), jnp.bfloat16),
        scratch_shapes=[
            pltpu.VMEM((2, HEAD_GROUP, tile + N_META, tile), jnp.float32),
            pltpu.VMEM((2, HEAD_GROUP, 1, tile), jnp.float32),
            pltpu.VMEM((tiles * N_HEADS, 1, tile), jnp.float32),
            pltpu.VMEM((tiles * N_HEADS, V_DIM + ONES_ROWS, tile), jnp.float32),
        ],
        compiler_params=pltpu.CompilerParams(
            dimension_semantics=("parallel", "parallel"), vmem_limit_bytes=VMEM_LIMIT),
        name="attention",
    )(*args)


def _post_attention_body(h_ref, a_ref, b_ref, gmix_ref, wgates_ref, wpa_ref, wpb_ref, wo_ref,
                         gffn_ref, wgate_ref, wup_ref, wdown_ref, gfin_ref, o_ref, *, final_norm):
    tile = h_ref.shape[0]
    step = min(tile, max(tile // POST_SPLIT, MXU_ROWS))
    parts = [pl.ds(r * step, step) for r in range(tile // step)]
    pa = [_dot(a_ref[r, :], wpa_ref[...]) for r in parts]
    pb = [_dot(b_ref[r, :], wpb_ref[...]) for r in parts]
    hs = [h_ref[r, :] for r in parts]
    hn = [_rmsnorm(h, gmix_ref[...]).astype(jnp.bfloat16) for h in hs]
    gates = [jax.nn.sigmoid(_dot(x, wgates_ref[...])) for x in hn]
    merged = [(g[:, :D_MODEL] * xa + g[:, D_MODEL:] * xb).astype(jnp.bfloat16)
              for g, xa, xb in zip(gates, pa, pb)]
    hs = [h + _dot(m, wo_ref[...]) for h, m in zip(hs, merged)]
    hn = [_rmsnorm(h, gffn_ref[...]).astype(jnp.bfloat16) for h in hs]
    act = [(jax.nn.silu(_dot(x, wgate_ref[...])) * _dot(x, wup_ref[...])).astype(jnp.bfloat16)
           for x in hn]
    hs = [h + _dot(x, wdown_ref[...]) for h, x in zip(hs, act)]
    for r, h in zip(parts, hs):
        o_ref[r, :] = _rmsnorm(h, gfin_ref[...]) if final_norm else h


def _post_attention(h, a, b, lw, layer, g_final, *, tile, final_norm):
    rows = h.shape[0]
    row_map = lambda r: (r, 0)
    body = functools.partial(_post_attention_body, final_norm=final_norm)
    return pl.pallas_call(
        body,
        grid=(rows // tile,),
        in_specs=[
            pl.BlockSpec((tile, D_MODEL), row_map),
            pl.BlockSpec((tile, POOL_WIDTH), row_map),
            pl.BlockSpec((tile, ATT_WIDTH), row_map),
            _const_spec((1, D_MODEL), layer),
            _const_spec((D_MODEL, 2 * D_MODEL), layer),
            _const_spec((POOL_WIDTH, D_MODEL), layer),
            _const_spec((ATT_WIDTH, D_MODEL), layer),
            _const_spec((D_MODEL, D_MODEL), layer),
            _const_spec((1, D_MODEL), layer),
            _const_spec((D_MODEL, D_FF), layer),
            _const_spec((D_MODEL, D_FF), layer),
            _const_spec((D_FF, D_MODEL), layer),
            _const_spec((1, D_MODEL)),
        ],
        out_specs=pl.BlockSpec((tile, D_MODEL), row_map),
        out_shape=jax.ShapeDtypeStruct((rows, D_MODEL), jnp.float32),
        compiler_params=pltpu.CompilerParams(
            dimension_semantics=("parallel",), vmem_limit_bytes=VMEM_LIMIT),
        name="post_attention",
    )(h, a, b, lw["g_mix"], lw["w_in_gates"], lw["w_pa"], lw["w_pb"], lw["w_o"], lw["g_ffn"],
      lw["w_gate"], lw["w_up"], lw["w_down"], g_final)


def _head_lanes():
    return [(QK_NOPE, QK_NOPE + ROPE_HALF), (QK_NOPE + ROPE_HALF, QK_DIM), (0, QK_NOPE),
            (None, HEAD_PAD - QK_DIM)]


def _place_head_columns(w, keep, xp=jnp):
    parts = []
    for start, stop in _head_lanes():
        if start is None or not keep(start):
            width = stop if start is None else stop - start
            parts.append(xp.zeros(w.shape[:-1] + (width,), w.dtype))
        else:
            parts.append(w[..., start:stop])
    return xp.concatenate(parts, axis=-1)


def _prepare_weights(norm_mix_g, w_in, pool_w, pool_scale, q_norm_g, kv_norm_g, w_uq, w_ukv,
                     w_pa, w_pb, w_o, norm_ffn_g, w_gate, w_up, w_down):
    bf = jnp.bfloat16
    depth = w_in.shape[0]
    o_q = POOL_WIDTH + Q_RANK
    o_kv = o_q + KV_RANK
    o_kr = o_kv + QK_ROPE
    is_nope = lambda col: col < QK_NOPE
    kr_cols = _place_head_columns(jnp.pad(w_in[..., o_kv:o_kr], ((0, 0), (0, 0), (QK_NOPE, 0))),
                                  lambda col: not is_nope(col))
    w_in_a = jnp.concatenate([w_in[..., :o_kv], kr_cols], axis=-1).astype(bf)
    w_q = _place_head_columns(w_uq.reshape(depth, Q_RANK, N_HEADS, QK_DIM), lambda col: True)
    ukv = w_ukv.reshape(depth, KV_RANK, N_HEADS, QK_NOPE + V_DIM)
    w_k = _place_head_columns(jnp.pad(ukv[..., :QK_NOPE], [(0, 0)] * 3 + [(0, QK_ROPE)]), is_nope)
    w_vt = jnp.swapaxes(ukv[..., QK_NOPE:].reshape(depth, KV_RANK, ATT_WIDTH), 1, 2)
    row = lambda g: g[:, None, :]
    return dict(
        g_mix=row(norm_mix_g), w_in_a=w_in_a, w_in_gates=w_in[..., o_kr:].astype(bf),
        pool_w=pool_w.astype(bf), pool_scale=row(pool_scale),
        q_g=row(q_norm_g), kv_g=row(kv_norm_g),
        w_qt=jnp.swapaxes(w_q.reshape(depth, Q_RANK, N_HEADS * HEAD_PAD), 1, 2).astype(bf),
        w_k=w_k.reshape(depth, KV_RANK, N_HEADS * HEAD_PAD).astype(bf), w_vt=w_vt.astype(bf),
        w_pa=w_pa.astype(bf), w_pb=w_pb.astype(bf), w_o=w_o.astype(bf),
        g_ffn=row(norm_ffn_g), w_gate=w_gate.astype(bf), w_up=w_up.astype(bf),
        w_down=w_down.astype(bf))


def _rope_tables(length):
    f32 = np.float32
    inv = f32(1.0) / (f32(ROPE_THETA) ** (np.arange(0, QK_ROPE, 2, dtype=f32) / f32(QK_ROPE)))
    ang = np.arange(length, dtype=f32)[:, None] * inv[None, :]
    cos, sin = np.cos(ang).astype(f32), np.sin(ang).astype(f32)
    ones = np.ones((length, QK_NOPE), f32)
    cos_t = np.asarray(_place_head_columns(np.concatenate([ones, cos, cos], axis=1),
                                           lambda col: True, xp=np))
    sin_t = np.asarray(_place_head_columns(np.concatenate([0 * ones, -sin, sin], axis=1),
                                           lambda col: True, xp=np))
    return cos_t * f32(Q_SCALE), sin_t * f32(Q_SCALE), cos_t, sin_t


def _tile_q_tables(tables, tile):
    cos_q, sin_q, cos_k, sin_k = tables
    per_tile = lambda t: np.ascontiguousarray(t.reshape(-1, tile, HEAD_PAD).transpose(0, 2, 1))
    return per_tile(cos_q), per_tile(sin_q), cos_k, sin_k


def kernel(x, meta_tokens, norm_mix_g, w_in, pool_w, pool_scale, q_norm_g, kv_norm_g, w_uq, w_ukv,
           w_pa, w_pb, w_o, norm_ffn_g, w_gate, w_up, w_down, final_norm_g):
    batch, seq, _ = x.shape
    depth = w_in.shape[0]
    tile = ROW_TILE
    tables = _rope_tables(N_META + max(seq, META_TILE))
    meta_tables = _tile_q_tables(tuple(t[:META_TILE] for t in tables), META_TILE)
    main_tables = _tile_q_tables(tuple(t[N_META:N_META + seq] for t in tables), tile)
    g_final = final_norm_g[None]
    lw = _prepare_weights(norm_mix_g, w_in, pool_w, pool_scale, q_norm_g, kv_norm_g, w_uq, w_ukv,
                          w_pa, w_pb, w_o, norm_ffn_g, w_gate, w_up, w_down)

    h = x.reshape(batch * seq, D_MODEL)
    hm = jnp.pad(meta_tokens.astype(x.dtype), ((0, META_TILE - N_META), (0, 0)))
    zero_halo = np.zeros((MAX_WINDOW, POOL_WIDTH), np.float32)
    for i in range(depth):
        last = i == depth - 1
        qm, km, vtm, am, um = _pre_attention(hm, lw, i, meta_tables, zero_halo, batch=1,
                                             seq=META_TILE, tile=META_TILE, pos_offset=0,
                                             emit_u=True)
        if not last:
            bm = _attention(qm, km, vtm, None, batch=1, seq=META_TILE, tile=META_TILE, tiles=1)
            hm = _post_attention(hm, am, bm, lw, i, g_final, tile=META_TILE, final_norm=False)
        q, k, vt, a = _pre_attention(h, lw, i, main_tables, um, batch=batch, seq=seq, tile=tile,
                                     pos_offset=N_META, emit_u=False)
        b = _attention(q, k, vt, (km, vtm), batch=batch, seq=seq, tile=tile, tiles=ATTN_TILES)
        h = _post_attention(h, a, b, lw, i, g_final, tile=tile, final_norm=last)
    return h.reshape(batch, seq, D_MODEL)
```

```python
import functools
import math

import jax
import jax.numpy as jnp
import numpy as np
from jax import lax
from jax.experimental import pallas as pl
from jax.experimental.pallas import tpu as pltpu

D_MODEL = 1024
N_META = 16
POOL_WINDOWS = (2, 4, 8, 16)
POOL_GROUP = 128
POOL_WIDTH = POOL_GROUP * len(POOL_WINDOWS)
N_HEADS = 16
QK_NOPE = 64
QK_ROPE = 32
V_DIM = 64
Q_RANK = 256
KV_RANK = 128
QK_DIM = QK_NOPE + QK_ROPE
ROPE_HALF = QK_ROPE // 2
ATT_WIDTH = N_HEADS * V_DIM
ROPE_THETA = 10000.0
D_FF = 2816
NORM_EPS = 1e-6
MASK_VALUE = -1e30

LANES = 128
HEAD_PAD = LANES
W_IN_A_WIDTH = POOL_WIDTH + Q_RANK + KV_RANK + HEAD_PAD
MAX_WINDOW = max(POOL_WINDOWS)
assert all(w & (w - 1) == 0 for w in POOL_WINDOWS)
POOL_PAD = 8
Q_SCALE = (QK_DIM ** -0.5) * math.log2(math.e)
PRE_TILE = 1024
ROW_TILE = 512
META_TILE = 128
HEAD_GROUP = 2
ONES_ROWS = 16
POST_SPLIT = 2
PRE_SPLIT = 4
Q_HEAD_CHUNK = 4
MXU_ROWS = 256
ATTN_TILES = 1
DIAG_BLOCK = 256
V7X_VMEM_BYTES = 64 * 1024 * 1024
VMEM_LIMIT = V7X_VMEM_BYTES * 7 // 8

_NT = (((1,), (1,)), ((), ()))


def _rmsnorm(x, g):
    return x * lax.rsqrt(jnp.mean(x * x, axis=-1, keepdims=True) + NORM_EPS) * g


def _dot(a, b):
    return jnp.dot(a, b, preferred_element_type=jnp.float32)


def _const_spec(shape, layer=None):
    zeros = (0,) * len(shape)
    if layer is None:
        return pl.BlockSpec(shape, lambda *_: zeros, pipeline_mode=pl.Buffered(1))
    return pl.BlockSpec((None,) + tuple(shape), lambda *_: (layer,) + zeros,
                        pipeline_mode=pl.Buffered(1))


def _pre_attention_body(h_ref, gmix_ref, win_ref, poolw_ref, pscale_ref, qg_ref, kvg_ref,
                        wqt_ref, wk_ref, wvt_ref, cosq_ref, sinq_ref, cosk_ref, sink_ref, halo_ref,
                        q_out, k_out, vt_out, a_out, *rest, tile, kv_tile, pos_offset, emit_u):
    if emit_u:
        u_out, pool_scr = rest
    else:
        (pool_scr,) = rest
    j = pl.program_id(1)
    row0 = pl.multiple_of(j * tile, tile)
    first, end = POOL_PAD + MAX_WINDOW, POOL_PAD + MAX_WINDOW + tile

    @pl.when(j == 0)
    def _():
        pool_scr[:, 0:POOL_PAD, :] = jnp.zeros((3, POOL_PAD, POOL_WIDTH), jnp.float32)
        pool_scr[0, POOL_PAD:first, :] = halo_ref[...]

    step = min(tile, max(tile // PRE_SPLIT, MXU_ROWS))
    parts = [r * step for r in range(tile // step)]
    hn = [_rmsnorm(h_ref[lo:lo + step, :], gmix_ref[...]).astype(jnp.bfloat16) for lo in parts]
    zs = [_dot(x, win_ref[:, POOL_WIDTH:]) for x in hn]
    us = [_dot(x, win_ref[:, 0:POOL_WIDTH]) for x in hn]

    cqn = [_rmsnorm(z[:, 0:Q_RANK], qg_ref[...]).astype(jnp.bfloat16) for z in zs]
    for lo, c in zip(parts, cqn):
        cos_q, sin_q = cosq_ref[j, :, lo:lo + step], sinq_ref[j, :, lo:lo + step]
        for h0 in range(0, N_HEADS, Q_HEAD_CHUNK):
            head_rows = slice(h0 * HEAD_PAD, (h0 + Q_HEAD_CHUNK) * HEAD_PAD)
            qt_raw = lax.dot_general(wqt_ref[head_rows, :], c, _NT,
                                     preferred_element_type=jnp.float32)
            for hd in range(Q_HEAD_CHUNK):
                x = qt_raw[hd * HEAD_PAD:(hd + 1) * HEAD_PAD, :]
                partner = jnp.concatenate([x[ROPE_HALF:QK_ROPE], x[:ROPE_HALF], x[QK_ROPE:]], axis=0)
                q_out[(h0 + hd) * HEAD_PAD:(h0 + hd + 1) * HEAD_PAD, lo:lo + step] = (
                    x * cos_q + partner * sin_q).astype(jnp.bfloat16)

    ckvn = [_rmsnorm(z[:, Q_RANK:Q_RANK + KV_RANK], kvg_ref[...]).astype(jnp.bfloat16) for z in zs]
    k_raws = [_dot(c, wk_ref[...]) for c in ckvn]
    for lo, z, k_raw in zip(parts, zs, k_raws):
        kr = z[:, Q_RANK + KV_RANK:]
        lane = lax.broadcasted_iota(jnp.int32, kr.shape, 1)
        kr_partner = jnp.where(lane < ROPE_HALF, pltpu.roll(kr, LANES - ROPE_HALF, 1),
                               pltpu.roll(kr, ROPE_HALF, 1))
        seq_rows = pl.ds(pl.multiple_of(row0 + lo, step), step)
        kr_roped = kr * cosk_ref[seq_rows, :] + kr_partner * sink_ref[seq_rows, :]
        for hd in range(N_HEADS):
            k_out[hd, lo:lo + step, :] = (
                k_raw[:, hd * HEAD_PAD:(hd + 1) * HEAD_PAD] + kr_roped).astype(jnp.bfloat16)
    for lo, c in zip(parts, ckvn):
        vt = lax.dot_general(wvt_ref[...], c, _NT, preferred_element_type=jnp.float32)
        at = lo % kv_tile
        vt_out[lo // kv_tile, :, at:at + step] = vt.astype(jnp.bfloat16)

    for lo, u in zip(parts, us):
        pool_scr[0, first + lo:first + lo + step, :] = u
        if emit_u:
            u_out[lo:lo + step, :] = u
    g1, g2, g3 = POOL_GROUP, 2 * POOL_GROUP, 3 * POOL_GROUP
    pool_scr[1, POOL_PAD:end, :] = pool_scr[0, POOL_PAD:end, :] + pool_scr[0, POOL_PAD - 1:end - 1, :]
    pool_scr[2, POOL_PAD:end, g1:] = (pool_scr[1, POOL_PAD:end, g1:]
                                      + pool_scr[1, POOL_PAD - 2:end - 2, g1:])
    pool_scr[1, POOL_PAD:end, g2:] = (pool_scr[2, POOL_PAD:end, g2:]
                                      + pool_scr[2, POOL_PAD - 4:end - 4, g2:])
    wsums = [pool_scr[1, first:end, 0:g1], pool_scr[2, first:end, g1:g2],
             pool_scr[1, first:end, g2:g3],
             pool_scr[1, first:end, g3:] + pool_scr[1, first - 8:end - 8, g3:]]
    outs = []
    for g, w in enumerate(POOL_WINDOWS):
        if pos_offset + 1 >= MAX_WINDOW:
            mean = wsums[g] * (1.0 / w)
        else:
            pos = pos_offset + row0 + lax.broadcasted_iota(jnp.int32, (tile, POOL_GROUP), 0)
            mean = wsums[g] / jnp.minimum(pos + 1, w).astype(jnp.float32)
        y = (mean - pool_scr[0, first:end, g * POOL_GROUP:(g + 1) * POOL_GROUP]).astype(jnp.bfloat16)
        outs.append(_dot(y, poolw_ref[g]))
    a = jnp.concatenate(outs, axis=1) * pscale_ref[...]
    a_out[...] = a.astype(jnp.bfloat16)
    pool_scr[0, POOL_PAD:first, :] = pool_scr[0, end - MAX_WINDOW:end, :]


def _pre_attention(h, lw, layer, tables, halo, *, batch, seq, tile, kv_tile, pos_offset, emit_u):
    rows = batch * seq
    n_tiles = seq // tile
    kv_per_tile = tile // kv_tile
    row_map = lambda b, j: (b * n_tiles + j, 0)
    in_specs = [
        pl.BlockSpec((tile, D_MODEL), row_map),
        _const_spec((1, D_MODEL), layer),
        _const_spec((D_MODEL, W_IN_A_WIDTH), layer),
        _const_spec((len(POOL_WINDOWS), POOL_GROUP, POOL_GROUP), layer),
        _const_spec((1, POOL_WIDTH), layer),
        _const_spec((1, Q_RANK), layer),
        _const_spec((1, KV_RANK), layer),
        _const_spec((N_HEADS * HEAD_PAD, Q_RANK), layer),
        _const_spec((KV_RANK, N_HEADS * HEAD_PAD), layer),
        _const_spec((ATT_WIDTH, KV_RANK), layer),
        _const_spec((n_tiles, HEAD_PAD, tile)),
        _const_spec((n_tiles, HEAD_PAD, tile)),
        _const_spec((seq, LANES)),
        _const_spec((seq, LANES)),
        pl.BlockSpec((MAX_WINDOW, POOL_WIDTH), lambda b, j: (0, 0)),
    ]
    out_shape = [
        jax.ShapeDtypeStruct((N_HEADS * HEAD_PAD, rows), jnp.bfloat16),
        jax.ShapeDtypeStruct((N_HEADS, rows, HEAD_PAD), jnp.bfloat16),
        jax.ShapeDtypeStruct((rows // kv_tile, ATT_WIDTH, kv_tile), jnp.bfloat16),
        jax.ShapeDtypeStruct((rows, POOL_WIDTH), jnp.bfloat16),
    ]
    out_specs = [
        pl.BlockSpec((N_HEADS * HEAD_PAD, tile), lambda b, j: (0, b * n_tiles + j)),
        pl.BlockSpec((N_HEADS, tile, HEAD_PAD), lambda b, j: (0, b * n_tiles + j, 0)),
        pl.BlockSpec((kv_per_tile, ATT_WIDTH, kv_tile), lambda b, j: (b * n_tiles + j, 0, 0)),
        pl.BlockSpec((tile, POOL_WIDTH), row_map),
    ]
    if emit_u:
        out_shape.append(jax.ShapeDtypeStruct((rows, POOL_WIDTH), jnp.float32))
        out_specs.append(pl.BlockSpec((tile, POOL_WIDTH), row_map))
    body = functools.partial(_pre_attention_body, tile=tile, kv_tile=kv_tile, pos_offset=pos_offset,
                             emit_u=emit_u)
    return pl.pallas_call(
        body,
        grid=(batch, n_tiles),
        in_specs=in_specs,
        out_specs=out_specs,
        out_shape=out_shape,
        scratch_shapes=[pltpu.VMEM((3, POOL_PAD + MAX_WINDOW + tile, POOL_WIDTH), jnp.float32)],
        compiler_params=pltpu.CompilerParams(
            dimension_semantics=("arbitrary", "arbitrary"), vmem_limit_bytes=VMEM_LIMIT),
        name="pre_attention",
    )(h, lw["g_mix"], lw["w_in_a"], lw["pool_w"], lw["pool_scale"], lw["q_g"], lw["kv_g"],
      lw["w_qt"], lw["w_k"], lw["w_vt"], *tables, halo)


def _attention_body(*refs, tile, tiles, diag, has_meta, group):
    if has_meta:
        q_ref, k_ref, vt_ref, tri_ref, km_ref, vtm_ref, o_ref, s_scr, smax_scr, m_scr, acc_scr = refs
    else:
        q_ref, k_ref, vt_ref, tri_ref, o_ref, s_scr, smax_scr, m_scr, acc_scr = refs
    step = pl.program_id(1)
    n_groups = N_HEADS // group
    n_sub = tile // diag

    def q_block(qt, hd, lo=0):
        return q_ref[hd * HEAD_PAD:(hd + 1) * HEAD_PAD, qt * tile + lo:(qt + 1) * tile]

    def produce_full(slot, qt, hg, t):
        k_rows = pl.ds(pl.multiple_of(t * tile, tile), tile)
        for g in range(group):
            hd = hg * group + g
            s = _dot(k_ref[hd, k_rows, :], q_block(qt, hd))
            s_scr[slot, g, 0:tile, 0:tile] = s
            smax_scr[slot, g] = jnp.max(s, axis=0, keepdims=True)

    def produce_diag(slot, qt, hg, t):
        for g in range(group):
            hd = hg * group + g
            smax = None
            for c in range(n_sub):
                lo = c * diag
                k = k_ref[hd, pl.ds(pl.multiple_of(t * tile + lo, diag), diag), :]
                if c == 0 and has_meta:
                    k = jnp.concatenate([k, km_ref[hd, 0:N_META, :]], axis=0)
                s = _dot(k, q_block(qt, hd, lo))
                on_diag = s[0:diag, 0:diag] + tri_ref[...]
                main = on_diag if lo + diag == tile else jnp.concatenate(
                    [on_diag, s[0:diag, diag:]], axis=1)
                s_scr[slot, g, lo:lo + diag, lo:tile] = main
                cmax = jnp.max(main, axis=0, keepdims=True)
                if c == 0 and has_meta:
                    s_scr[slot, g, tile:tile + N_META, 0:tile] = s[diag:, :]
                    cmax = jnp.maximum(cmax, jnp.max(s[diag:, :], axis=0, keepdims=True))
                smax = cmax if c == 0 else jnp.concatenate(
                    [smax[:, :lo], jnp.maximum(smax[:, lo:], cmax)], axis=1)
            smax_scr[slot, g] = smax

    def consume(slot, qt, hg, t, is_diag):
        for g in range(group):
            hd = hg * group + g
            state = qt * N_HEADS + hd
            v_rows = pl.ds(pl.multiple_of(hd * V_DIM, V_DIM), V_DIM)
            m_old = m_scr[state]
            m_new = jnp.maximum(m_old, smax_scr[slot, g])
            if not is_diag:
                p = jnp.exp2(s_scr[slot, g, 0:tile, 0:tile] - m_new).astype(jnp.bfloat16)
                ones = jnp.ones((ONES_ROWS, tile), jnp.bfloat16)
                pv = _dot(jnp.concatenate([vt_ref[t, v_rows, :], ones], axis=0), p)
            else:
                ones = jnp.ones((ONES_ROWS, diag), jnp.bfloat16)
                for c in range(n_sub):
                    lo = c * diag
                    p = jnp.exp2(s_scr[slot, g, lo:lo + diag, lo:tile] - m_new[:, lo:]).astype(jnp.bfloat16)
                    part = _dot(jnp.concatenate([vt_ref[t, v_rows, lo:lo + diag], ones], axis=0), p)
                    pv = part if c == 0 else jnp.concatenate(
                        [pv[:, :lo], pv[:, lo:] + part], axis=1)
                if has_meta:
                    pm = jnp.exp2(s_scr[slot, g, tile:tile + N_META, 0:tile] - m_new).astype(jnp.bfloat16)
                    ones_meta = jnp.ones((ONES_ROWS, N_META), jnp.bfloat16)
                    pv = pv + _dot(
                        jnp.concatenate([vtm_ref[0, v_rows, 0:N_META], ones_meta], axis=0), pm)
            acc_scr[state] = jnp.exp2(m_old - m_new) * acc_scr[state] + pv
            m_scr[state] = m_new

    def finalize_pair(qt, pair):
        acc0, acc1 = acc_scr[qt * N_HEADS + 2 * pair], acc_scr[qt * N_HEADS + 2 * pair + 1]
        both = jnp.concatenate([acc0[0:V_DIM] / acc0[V_DIM:V_DIM + 1],
                                acc1[0:V_DIM] / acc1[V_DIM:V_DIM + 1]], axis=0)
        o_ref[qt * tile:(qt + 1) * tile, pair * LANES:(pair + 1) * LANES] = both.T.astype(jnp.bfloat16)

    def key_block(qt, t, is_diag, produce_next):
        for hg in range(n_groups):
            if hg + 1 < n_groups:
                (produce_diag if is_diag else produce_full)((hg + 1) % 2, qt, hg + 1, t)
            elif produce_next is not None:
                produce_next()
            consume(hg % 2, qt, hg, t, is_diag)
            if is_diag:
                for hd in range(hg * group, (hg + 1) * group, 2):
                    finalize_pair(qt, hd // 2)

    m_scr[...] = jnp.full_like(m_scr, MASK_VALUE)
    acc_scr[...] = jnp.zeros_like(acc_scr)

    for qt in range(tiles):
        i = step * tiles + qt
        first_of_next = (lambda qt=qt: produce_full(0, qt + 1, 0, 0)) if qt + 1 < tiles else None

        def full_blocks_then_diag(qt=qt, i=i, first_of_next=first_of_next):
            def full_block(t, carry):
                key_block(qt, t, False, lambda: produce_full(0, qt, 0, t + 1))
                return carry
            lax.fori_loop(0, i - 1, full_block, 0)
            key_block(qt, i - 1, False, lambda: produce_diag(0, qt, 0, i))
            key_block(qt, i, True, first_of_next)

        if qt > 0:
            full_blocks_then_diag()
        else:
            @pl.when(i == 0)
            def _(qt=qt, i=i, first_of_next=first_of_next):
                produce_diag(0, qt, 0, i)
                key_block(qt, i, True, first_of_next)

            @pl.when(i > 0)
            def _(qt=qt, run=full_blocks_then_diag):
                produce_full(0, qt, 0, 0)
                run()


def _attention(q, k, vt, meta_kv, *, batch, seq, tile, tiles):
    rows = batch * seq
    nq = seq // tile
    steps = nq // tiles
    has_meta = meta_kv is not None
    assert (N_HEADS // HEAD_GROUP) % 2 == 0 and HEAD_GROUP % 2 == 0 and nq % tiles == 0
    diag = min(tile, DIAG_BLOCK)
    idx = np.arange(diag)
    tri = np.where(idx[:, None] <= idx[None, :], 0.0, MASK_VALUE).astype(np.float32)
    in_specs = [
        pl.BlockSpec((N_HEADS * HEAD_PAD, tiles * tile), lambda b, i: (0, b * steps + i)),
        pl.BlockSpec((N_HEADS, seq, HEAD_PAD), lambda b, i: (0, b, 0)),
        pl.BlockSpec((nq, ATT_WIDTH, tile), lambda b, i: (b, 0, 0)),
        _const_spec((diag, diag)),
    ]
    args = [q, k, vt, tri]
    if has_meta:
        km, vtm = meta_kv
        in_specs += [_const_spec(km.shape), _const_spec(vtm.shape)]
        args += [km, vtm]
    body = functools.partial(_attention_body, tile=tile, tiles=tiles, diag=diag, has_meta=has_meta,
                             group=HEAD_GROUP)
    return pl.pallas_call(
        body,
        grid=(batch, steps),
        in_specs=in_specs,
        out_specs=pl.BlockSpec((tiles * tile, ATT_WIDTH), lambda b, i: (b * steps + i, 0)),
        out_shape=jax.ShapeDtypeStruct((rows, ATT_WIDTH), jnp.bfloat16),
        scratch_shapes=[
            pltpu.VMEM((2, HEAD_GROUP, tile + N_META, tile), jnp.float32),
            pltpu.VMEM((2, HEAD_GROUP, 1, tile), jnp.float32),
            pltpu.VMEM((tiles * N_HEADS, 1, tile), jnp.float32),
            pltpu.VMEM((tiles * N_HEADS, V_DIM + ONES_ROWS, tile), jnp.float32),
        ],
        compiler_params=pltpu.CompilerParams(
            dimension_semantics=("parallel", "parallel"), vmem_limit_bytes=VMEM_LIMIT),
        name="attention",
    )(*args)


def _post_attention_body(h_ref, a_ref, b_ref, gmix_ref, wgates_ref, wpa_ref, wpb_ref, wo_ref,
                         gffn_ref, wgate_ref, wup_ref, wdown_ref, gfin_ref, o_ref, *, final_norm):
    tile = h_ref.shape[0]
    step = min(tile, max(tile // POST_SPLIT, MXU_ROWS))
    parts = [pl.ds(r * step, step) for r in range(tile // step)]
    pa = [_dot(a_ref[r, :], wpa_ref[...]) for r in parts]
    pb = [_dot(b_ref[r, :], wpb_ref[...]) for r in parts]
    hs = [h_ref[r, :] for r in parts]
    hn = [_rmsnorm(h, gmix_ref[...]).astype(jnp.bfloat16) for h in hs]
    gates = [jax.nn.sigmoid(_dot(x, wgates_ref[...])) for x in hn]
    merged = [(g[:, :D_MODEL] * xa + g[:, D_MODEL:] * xb).astype(jnp.bfloat16)
              for g, xa, xb in zip(gates, pa, pb)]
    hs = [h + _dot(m, wo_ref[...]) for h, m in zip(hs, merged)]
    hn = [_rmsnorm(h, gffn_ref[...]).astype(jnp.bfloat16) for h in hs]
    act = [(jax.nn.silu(_dot(x, wgate_ref[...])) * _dot(x, wup_ref[...])).astype(jnp.bfloat16)
           for x in hn]
    hs = [h + _dot(x, wdown_ref[...]) for h, x in zip(hs, act)]
    for r, h in zip(parts, hs):
        o_ref[r, :] = _rmsnorm(h, gfin_ref[...]) if final_norm else h


def _post_attention(h, a, b, lw, layer, g_final, *, tile, final_norm):
    rows = h.shape[0]
    row_map = lambda r: (r, 0)
    body = functools.partial(_post_attention_body, final_norm=final_norm)
    return pl.pallas_call(
        body,
        grid=(rows // tile,),
        in_specs=[
            pl.BlockSpec((tile, D_MODEL), row_map),
            pl.BlockSpec((tile, POOL_WIDTH), row_map),
            pl.BlockSpec((tile, ATT_WIDTH), row_map),
            _const_spec((1, D_MODEL), layer),
            _const_spec((D_MODEL, 2 * D_MODEL), layer),
            _const_spec((POOL_WIDTH, D_MODEL), layer),
            _const_spec((ATT_WIDTH, D_MODEL), layer),
            _const_spec((D_MODEL, D_MODEL), layer),
            _const_spec((1, D_MODEL), layer),
            _const_spec((D_MODEL, D_FF), layer),
            _const_spec((D_MODEL, D_FF), layer),
            _const_spec((D_FF, D_MODEL), layer),
            _const_spec((1, D_MODEL)),
        ],
        out_specs=pl.BlockSpec((tile, D_MODEL), row_map),
        out_shape=jax.ShapeDtypeStruct((rows, D_MODEL), jnp.float32),
        compiler_params=pltpu.CompilerParams(
            dimension_semantics=("parallel",), vmem_limit_bytes=VMEM_LIMIT),
        name="post_attention",
    )(h, a, b, lw["g_mix"], lw["w_in_gates"], lw["w_pa"], lw["w_pb"], lw["w_o"], lw["g_ffn"],
      lw["w_gate"], lw["w_up"], lw["w_down"], g_final)


def _head_lanes():
    return [(QK_NOPE, QK_NOPE + ROPE_HALF), (QK_NOPE + ROPE_HALF, QK_DIM), (0, QK_NOPE),
            (None, HEAD_PAD - QK_DIM)]


def _place_head_columns(w, keep, xp=jnp):
    parts = []
    for start, stop in _head_lanes():
        if start is None or not keep(start):
            width = stop if start is None else stop - start
            parts.append(xp.zeros(w.shape[:-1] + (width,), w.dtype))
        else:
            parts.append(w[..., start:stop])
    return xp.concatenate(parts, axis=-1)


def _prepare_weights(norm_mix_g, w_in, pool_w, pool_scale, q_norm_g, kv_norm_g, w_uq, w_ukv,
                     w_pa, w_pb, w_o, norm_ffn_g, w_gate, w_up, w_down):
    bf = jnp.bfloat16
    depth = w_in.shape[0]
    o_q = POOL_WIDTH + Q_RANK
    o_kv = o_q + KV_RANK
    o_kr = o_kv + QK_ROPE
    is_nope = lambda col: col < QK_NOPE
    kr_cols = _place_head_columns(jnp.pad(w_in[..., o_kv:o_kr], ((0, 0), (0, 0), (QK_NOPE, 0))),
                                  lambda col: not is_nope(col))
    w_in_a = jnp.concatenate([w_in[..., :o_kv], kr_cols], axis=-1).astype(bf)
    w_q = _place_head_columns(w_uq.reshape(depth, Q_RANK, N_HEADS, QK_DIM), lambda col: True)
    ukv = w_ukv.reshape(depth, KV_RANK, N_HEADS, QK_NOPE + V_DIM)
    w_k = _place_head_columns(jnp.pad(ukv[..., :QK_NOPE], [(0, 0)] * 3 + [(0, QK_ROPE)]), is_nope)
    w_vt = jnp.swapaxes(ukv[..., QK_NOPE:].reshape(depth, KV_RANK, ATT_WIDTH), 1, 2)
    row = lambda g: g[:, None, :]
    return dict(
        g_mix=row(norm_mix_g), w_in_a=w_in_a, w_in_gates=w_in[..., o_kr:].astype(bf),
        pool_w=pool_w.astype(bf), pool_scale=row(pool_scale),
        q_g=row(q_norm_g), kv_g=row(kv_norm_g),
        w_qt=jnp.swapaxes(w_q.reshape(depth, Q_RANK, N_HEADS * HEAD_PAD), 1, 2).astype(bf),
        w_k=w_k.reshape(depth, KV_RANK, N_HEADS * HEAD_PAD).astype(bf), w_vt=w_vt.astype(bf),
        w_pa=w_pa.astype(bf), w_pb=w_pb.astype(bf), w_o=w_o.astype(bf),
        g_ffn=row(norm_ffn_g), w_gate=w_gate.astype(bf), w_up=w_up.astype(bf),
        w_down=w_down.astype(bf))


def _rope_tables(length):
    f32 = np.float32
    inv = f32(1.0) / (f32(ROPE_THETA) ** (np.arange(0, QK_ROPE, 2, dtype=f32) / f32(QK_ROPE)))
    ang = np.arange(length, dtype=f32)[:, None] * inv[None, :]
    cos, sin = np.cos(ang).astype(f32), np.sin(ang).astype(f32)
    ones = np.ones((length, QK_NOPE), f32)
    cos_t = np.asarray(_place_head_columns(np.concatenate([ones, cos, cos], axis=1),
                                           lambda col: True, xp=np))
    sin_t = np.asarray(_place_head_columns(np.concatenate([0 * ones, -sin, sin], axis=1),
                                           lambda col: True, xp=np))
    return cos_t * f32(Q_SCALE), sin_t * f32(Q_SCALE), cos_t, sin_t


def _tile_q_tables(tables, tile):
    cos_q, sin_q, cos_k, sin_k = tables
    per_tile = lambda t: np.ascontiguousarray(t.reshape(-1, tile, HEAD_PAD).transpose(0, 2, 1))
    return per_tile(cos_q), per_tile(sin_q), cos_k, sin_k


def kernel(x, meta_tokens, norm_mix_g, w_in, pool_w, pool_scale, q_norm_g, kv_norm_g, w_uq, w_ukv,
           w_pa, w_pb, w_o, norm_ffn_g, w_gate, w_up, w_down, final_norm_g):
    batch, seq, _ = x.shape
    depth = w_in.shape[0]
    tile = ROW_TILE
    tables = _rope_tables(N_META + max(seq, META_TILE))
    meta_tables = _tile_q_tables(tuple(t[:META_TILE] for t in tables), META_TILE)
    main_tables = _tile_q_tables(tuple(t[N_META:N_META + seq] for t in tables), PRE_TILE)
    g_final = final_norm_g[None]
    lw = _prepare_weights(norm_mix_g, w_in, pool_w, pool_scale, q_norm_g, kv_norm_g, w_uq, w_ukv,
                          w_pa, w_pb, w_o, norm_ffn_g, w_gate, w_up, w_down)

    h = x.reshape(batch * seq, D_MODEL)
    hm = jnp.pad(meta_tokens.astype(x.dtype), ((0, META_TILE - N_META), (0, 0)))
    zero_halo = np.zeros((MAX_WINDOW, POOL_WIDTH), np.float32)
    for i in range(depth):
        last = i == depth - 1
        qm, km, vtm, am, um = _pre_attention(hm, lw, i, meta_tables, zero_halo, batch=1,
                                             seq=META_TILE, tile=META_TILE, kv_tile=META_TILE,
                                             pos_offset=0,
                                             emit_u=True)
        if not last:
            bm = _attention(qm, km, vtm, None, batch=1, seq=META_TILE, tile=META_TILE, tiles=1)
            hm = _post_attention(hm, am, bm, lw, i, g_final, tile=META_TILE, final_norm=False)
        q, k, vt, a = _pre_attention(h, lw, i, main_tables, um, batch=batch, seq=seq,
                                     tile=PRE_TILE, kv_tile=tile, pos_offset=N_META, emit_u=False)
        b = _attention(q, k, vt, (km, vtm), batch=batch, seq=seq, tile=tile, tiles=ATTN_TILES)
        h = _post_attention(h, a, b, lw, i, g_final, tile=tile, final_norm=last)
    return h.reshape(batch, seq, D_MODEL)
```

```python
import functools
import math

import jax
import jax.numpy as jnp
import numpy as np
from jax import lax
from jax.experimental import pallas as pl
from jax.experimental.pallas import tpu as pltpu

D_MODEL = 1024
N_META = 16
POOL_WINDOWS = (2, 4, 8, 16)
POOL_GROUP = 128
POOL_WIDTH = POOL_GROUP * len(POOL_WINDOWS)
N_HEADS = 16
QK_NOPE = 64
QK_ROPE = 32
V_DIM = 64
Q_RANK = 256
KV_RANK = 128
QK_DIM = QK_NOPE + QK_ROPE
ROPE_HALF = QK_ROPE // 2
ATT_WIDTH = N_HEADS * V_DIM
ROPE_THETA = 10000.0
D_FF = 2816
NORM_EPS = 1e-6
MASK_VALUE = -1e30

LANES = 128
HEAD_PAD = LANES
W_IN_A_WIDTH = POOL_WIDTH + Q_RANK + KV_RANK + HEAD_PAD
MAX_WINDOW = max(POOL_WINDOWS)
assert all(w & (w - 1) == 0 for w in POOL_WINDOWS)
POOL_PAD = 8
Q_SCALE = (QK_DIM ** -0.5) * math.log2(math.e)
PRE_TILE = 1024
ROW_TILE = 512
ATTN_TILE = 256
META_TILE = 128
HEAD_GROUP = 4
ONES_ROWS = 16
POST_SPLIT = 2
PRE_SPLIT = 4
Q_HEAD_CHUNK = 4
MXU_ROWS = 256
ATTN_TILES = 1
DIAG_BLOCK = 256
V7X_VMEM_BYTES = 64 * 1024 * 1024
VMEM_LIMIT = V7X_VMEM_BYTES * 7 // 8

_NT = (((1,), (1,)), ((), ()))


def _rmsnorm(x, g):
    return x * lax.rsqrt(jnp.mean(x * x, axis=-1, keepdims=True) + NORM_EPS) * g


def _dot(a, b):
    return jnp.dot(a, b, preferred_element_type=jnp.float32)


def _const_spec(shape, layer=None):
    zeros = (0,) * len(shape)
    if layer is None:
        return pl.BlockSpec(shape, lambda *_: zeros, pipeline_mode=pl.Buffered(1))
    return pl.BlockSpec((None,) + tuple(shape), lambda *_: (layer,) + zeros,
                        pipeline_mode=pl.Buffered(1))


def _pre_attention_body(h_ref, gmix_ref, win_ref, poolw_ref, pscale_ref, qg_ref, kvg_ref,
                        wqt_ref, wk_ref, wvt_ref, cosq_ref, sinq_ref, cosk_ref, sink_ref, halo_ref,
                        q_out, k_out, vt_out, a_out, *rest, tile, kv_tile, pos_offset, emit_u):
    if emit_u:
        u_out, pool_scr = rest
    else:
        (pool_scr,) = rest
    j = pl.program_id(1)
    row0 = pl.multiple_of(j * tile, tile)
    first, end = POOL_PAD + MAX_WINDOW, POOL_PAD + MAX_WINDOW + tile

    @pl.when(j == 0)
    def _():
        pool_scr[:, 0:POOL_PAD, :] = jnp.zeros((3, POOL_PAD, POOL_WIDTH), jnp.float32)
        pool_scr[0, POOL_PAD:first, :] = halo_ref[...]

    step = min(tile, max(tile // PRE_SPLIT, MXU_ROWS))
    parts = [r * step for r in range(tile // step)]
    hn = [_rmsnorm(h_ref[lo:lo + step, :], gmix_ref[...]).astype(jnp.bfloat16) for lo in parts]
    zs = [_dot(x, win_ref[:, POOL_WIDTH:]) for x in hn]
    us = [_dot(x, win_ref[:, 0:POOL_WIDTH]) for x in hn]

    cqn = [_rmsnorm(z[:, 0:Q_RANK], qg_ref[...]).astype(jnp.bfloat16) for z in zs]
    for lo, c in zip(parts, cqn):
        cos_q, sin_q = cosq_ref[j, :, lo:lo + step], sinq_ref[j, :, lo:lo + step]
        for h0 in range(0, N_HEADS, Q_HEAD_CHUNK):
            head_rows = slice(h0 * HEAD_PAD, (h0 + Q_HEAD_CHUNK) * HEAD_PAD)
            qt_raw = lax.dot_general(wqt_ref[head_rows, :], c, _NT,
                                     preferred_element_type=jnp.float32)
            for hd in range(Q_HEAD_CHUNK):
                x = qt_raw[hd * HEAD_PAD:(hd + 1) * HEAD_PAD, :]
                partner = jnp.concatenate([x[ROPE_HALF:QK_ROPE], x[:ROPE_HALF], x[QK_ROPE:]], axis=0)
                q_out[(h0 + hd) * HEAD_PAD:(h0 + hd + 1) * HEAD_PAD, lo:lo + step] = (
                    x * cos_q + partner * sin_q).astype(jnp.bfloat16)

    ckvn = [_rmsnorm(z[:, Q_RANK:Q_RANK + KV_RANK], kvg_ref[...]).astype(jnp.bfloat16) for z in zs]
    k_raws = [_dot(c, wk_ref[...]) for c in ckvn]
    for lo, z, k_raw in zip(parts, zs, k_raws):
        kr = z[:, Q_RANK + KV_RANK:]
        lane = lax.broadcasted_iota(jnp.int32, kr.shape, 1)
        kr_partner = jnp.where(lane < ROPE_HALF, pltpu.roll(kr, LANES - ROPE_HALF, 1),
                               pltpu.roll(kr, ROPE_HALF, 1))
        seq_rows = pl.ds(pl.multiple_of(row0 + lo, step), step)
        kr_roped = kr * cosk_ref[seq_rows, :] + kr_partner * sink_ref[seq_rows, :]
        for hd in range(N_HEADS):
            k_out[hd, lo:lo + step, :] = (
                k_raw[:, hd * HEAD_PAD:(hd + 1) * HEAD_PAD] + kr_roped).astype(jnp.bfloat16)
    for lo, c in zip(parts, ckvn):
        vt = lax.dot_general(wvt_ref[...], c, _NT, preferred_element_type=jnp.float32)
        at = lo % kv_tile
        vt_out[lo // kv_tile, :, at:at + step] = vt.astype(jnp.bfloat16)

    for lo, u in zip(parts, us):
        pool_scr[0, first + lo:first + lo + step, :] = u
        if emit_u:
            u_out[lo:lo + step, :] = u
    g1, g2, g3 = POOL_GROUP, 2 * POOL_GROUP, 3 * POOL_GROUP
    pool_scr[1, POOL_PAD:end, :] = pool_scr[0, POOL_PAD:end, :] + pool_scr[0, POOL_PAD - 1:end - 1, :]
    pool_scr[2, POOL_PAD:end, g1:] = (pool_scr[1, POOL_PAD:end, g1:]
                                      + pool_scr[1, POOL_PAD - 2:end - 2, g1:])
    pool_scr[1, POOL_PAD:end, g2:] = (pool_scr[2, POOL_PAD:end, g2:]
                                      + pool_scr[2, POOL_PAD - 4:end - 4, g2:])
    wsums = [pool_scr[1, first:end, 0:g1], pool_scr[2, first:end, g1:g2],
             pool_scr[1, first:end, g2:g3],
             pool_scr[1, first:end, g3:] + pool_scr[1, first - 8:end - 8, g3:]]
    outs = []
    for g, w in enumerate(POOL_WINDOWS):
        if pos_offset + 1 >= MAX_WINDOW:
            mean = wsums[g] * (1.0 / w)
        else:
            pos = pos_offset + row0 + lax.broadcasted_iota(jnp.int32, (tile, POOL_GROUP), 0)
            mean = wsums[g] / jnp.minimum(pos + 1, w).astype(jnp.float32)
        y = (mean - pool_scr[0, first:end, g * POOL_GROUP:(g + 1) * POOL_GROUP]).astype(jnp.bfloat16)
        outs.append(_dot(y, poolw_ref[g]))
    a = jnp.concatenate(outs, axis=1) * pscale_ref[...]
    a_out[...] = a.astype(jnp.bfloat16)
    pool_scr[0, POOL_PAD:first, :] = pool_scr[0, end - MAX_WINDOW:end, :]


def _pre_attention(h, lw, layer, tables, halo, *, batch, seq, tile, kv_tile, pos_offset, emit_u):
    rows = batch * seq
    n_tiles = seq // tile
    kv_per_tile = tile // kv_tile
    row_map = lambda b, j: (b * n_tiles + j, 0)
    in_specs = [
        pl.BlockSpec((tile, D_MODEL), row_map),
        _const_spec((1, D_MODEL), layer),
        _const_spec((D_MODEL, W_IN_A_WIDTH), layer),
        _const_spec((len(POOL_WINDOWS), POOL_GROUP, POOL_GROUP), layer),
        _const_spec((1, POOL_WIDTH), layer),
        _const_spec((1, Q_RANK), layer),
        _const_spec((1, KV_RANK), layer),
        _const_spec((N_HEADS * HEAD_PAD, Q_RANK), layer),
        _const_spec((KV_RANK, N_HEADS * HEAD_PAD), layer),
        _const_spec((ATT_WIDTH, KV_RANK), layer),
        _const_spec((n_tiles, HEAD_PAD, tile)),
        _const_spec((n_tiles, HEAD_PAD, tile)),
        _const_spec((seq, LANES)),
        _const_spec((seq, LANES)),
        pl.BlockSpec((MAX_WINDOW, POOL_WIDTH), lambda b, j: (0, 0)),
    ]
    out_shape = [
        jax.ShapeDtypeStruct((N_HEADS * HEAD_PAD, rows), jnp.bfloat16),
        jax.ShapeDtypeStruct((N_HEADS, rows, HEAD_PAD), jnp.bfloat16),
        jax.ShapeDtypeStruct((rows // kv_tile, ATT_WIDTH, kv_tile), jnp.bfloat16),
        jax.ShapeDtypeStruct((rows, POOL_WIDTH), jnp.bfloat16),
    ]
    out_specs = [
        pl.BlockSpec((N_HEADS * HEAD_PAD, tile), lambda b, j: (0, b * n_tiles + j)),
        pl.BlockSpec((N_HEADS, tile, HEAD_PAD), lambda b, j: (0, b * n_tiles + j, 0)),
        pl.BlockSpec((kv_per_tile, ATT_WIDTH, kv_tile), lambda b, j: (b * n_tiles + j, 0, 0)),
        pl.BlockSpec((tile, POOL_WIDTH), row_map),
    ]
    if emit_u:
        out_shape.append(jax.ShapeDtypeStruct((rows, POOL_WIDTH), jnp.float32))
        out_specs.append(pl.BlockSpec((tile, POOL_WIDTH), row_map))
    body = functools.partial(_pre_attention_body, tile=tile, kv_tile=kv_tile, pos_offset=pos_offset,
                             emit_u=emit_u)
    return pl.pallas_call(
        body,
        grid=(batch, n_tiles),
        in_specs=in_specs,
        out_specs=out_specs,
        out_shape=out_shape,
        scratch_shapes=[pltpu.VMEM((3, POOL_PAD + MAX_WINDOW + tile, POOL_WIDTH), jnp.float32)],
        compiler_params=pltpu.CompilerParams(
            dimension_semantics=("arbitrary", "arbitrary"), vmem_limit_bytes=VMEM_LIMIT),
        name="pre_attention",
    )(h, lw["g_mix"], lw["w_in_a"], lw["pool_w"], lw["pool_scale"], lw["q_g"], lw["kv_g"],
      lw["w_qt"], lw["w_k"], lw["w_vt"], *tables, halo)


def _attention_body(*refs, tile, tiles, diag, has_meta, group):
    if has_meta:
        q_ref, k_ref, vt_ref, tri_ref, km_ref, vtm_ref, o_ref, s_scr, smax_scr, m_scr, acc_scr = refs
    else:
        q_ref, k_ref, vt_ref, tri_ref, o_ref, s_scr, smax_scr, m_scr, acc_scr = refs
    step = pl.program_id(1)
    n_groups = N_HEADS // group
    n_sub = tile // diag

    def q_block(qt, hd, lo=0):
        return q_ref[hd * HEAD_PAD:(hd + 1) * HEAD_PAD, qt * tile + lo:(qt + 1) * tile]

    def produce_full(slot, qt, hg, t):
        k_rows = pl.ds(pl.multiple_of(t * tile, tile), tile)
        for g in range(group):
            hd = hg * group + g
            s = _dot(k_ref[hd, k_rows, :], q_block(qt, hd))
            s_scr[slot, g, 0:tile, 0:tile] = s
            smax_scr[slot, g] = jnp.max(s, axis=0, keepdims=True)

    def produce_diag(slot, qt, hg, t):
        for g in range(group):
            hd = hg * group + g
            smax = None
            for c in range(n_sub):
                lo = c * diag
                k = k_ref[hd, pl.ds(pl.multiple_of(t * tile + lo, diag), diag), :]
                if c == 0 and has_meta:
                    k = jnp.concatenate([k, km_ref[hd, 0:N_META, :]], axis=0)
                s = _dot(k, q_block(qt, hd, lo))
                on_diag = s[0:diag, 0:diag] + tri_ref[...]
                main = on_diag if lo + diag == tile else jnp.concatenate(
                    [on_diag, s[0:diag, diag:]], axis=1)
                s_scr[slot, g, lo:lo + diag, lo:tile] = main
                cmax = jnp.max(main, axis=0, keepdims=True)
                if c == 0 and has_meta:
                    s_scr[slot, g, tile:tile + N_META, 0:tile] = s[diag:, :]
                    cmax = jnp.maximum(cmax, jnp.max(s[diag:, :], axis=0, keepdims=True))
                smax = cmax if c == 0 else jnp.concatenate(
                    [smax[:, :lo], jnp.maximum(smax[:, lo:], cmax)], axis=1)
            smax_scr[slot, g] = smax

    def consume(slot, qt, hg, t, is_diag):
        for g in range(group):
            hd = hg * group + g
            state = qt * N_HEADS + hd
            v_rows = pl.ds(pl.multiple_of(hd * V_DIM, V_DIM), V_DIM)
            m_old = m_scr[state]
            m_new = jnp.maximum(m_old, smax_scr[slot, g])
            if not is_diag:
                p = jnp.exp2(s_scr[slot, g, 0:tile, 0:tile] - m_new).astype(jnp.bfloat16)
                ones = jnp.ones((ONES_ROWS, tile), jnp.bfloat16)
                pv = _dot(jnp.concatenate([vt_ref[t, v_rows, :], ones], axis=0), p)
            else:
                ones = jnp.ones((ONES_ROWS, diag), jnp.bfloat16)
                for c in range(n_sub):
                    lo = c * diag
                    p = jnp.exp2(s_scr[slot, g, lo:lo + diag, lo:tile] - m_new[:, lo:]).astype(jnp.bfloat16)
                    part = _dot(jnp.concatenate([vt_ref[t, v_rows, lo:lo + diag], ones], axis=0), p)
                    pv = part if c == 0 else jnp.concatenate(
                        [pv[:, :lo], pv[:, lo:] + part], axis=1)
                if has_meta:
                    pm = jnp.exp2(s_scr[slot, g, tile:tile + N_META, 0:tile] - m_new).astype(jnp.bfloat16)
                    ones_meta = jnp.ones((ONES_ROWS, N_META), jnp.bfloat16)
                    pv = pv + _dot(
                        jnp.concatenate([vtm_ref[0, v_rows, 0:N_META], ones_meta], axis=0), pm)
            acc_scr[state] = jnp.exp2(m_old - m_new) * acc_scr[state] + pv
            m_scr[state] = m_new

    def finalize_pair(qt, pair):
        acc0, acc1 = acc_scr[qt * N_HEADS + 2 * pair], acc_scr[qt * N_HEADS + 2 * pair + 1]
        both = jnp.concatenate([acc0[0:V_DIM] / acc0[V_DIM:V_DIM + 1],
                                acc1[0:V_DIM] / acc1[V_DIM:V_DIM + 1]], axis=0)
        o_ref[qt * tile:(qt + 1) * tile, pair * LANES:(pair + 1) * LANES] = both.T.astype(jnp.bfloat16)

    def key_block(qt, t, is_diag, produce_next):
        for hg in range(n_groups):
            if hg + 1 < n_groups:
                (produce_diag if is_diag else produce_full)((hg + 1) % 2, qt, hg + 1, t)
            elif produce_next is not None:
                produce_next()
            consume(hg % 2, qt, hg, t, is_diag)
            if is_diag:
                for hd in range(hg * group, (hg + 1) * group, 2):
                    finalize_pair(qt, hd // 2)

    m_scr[...] = jnp.full_like(m_scr, MASK_VALUE)
    acc_scr[...] = jnp.zeros_like(acc_scr)

    for qt in range(tiles):
        i = step * tiles + qt
        first_of_next = (lambda qt=qt: produce_full(0, qt + 1, 0, 0)) if qt + 1 < tiles else None

        def full_blocks_then_diag(qt=qt, i=i, first_of_next=first_of_next):
            def full_block(t, carry):
                key_block(qt, t, False, lambda: produce_full(0, qt, 0, t + 1))
                return carry
            lax.fori_loop(0, i - 1, full_block, 0)
            key_block(qt, i - 1, False, lambda: produce_diag(0, qt, 0, i))
            key_block(qt, i, True, first_of_next)

        if qt > 0:
            full_blocks_then_diag()
        else:
            @pl.when(i == 0)
            def _(qt=qt, i=i, first_of_next=first_of_next):
                produce_diag(0, qt, 0, i)
                key_block(qt, i, True, first_of_next)

            @pl.when(i > 0)
            def _(qt=qt, run=full_blocks_then_diag):
                produce_full(0, qt, 0, 0)
                run()


def _attention(q, k, vt, meta_kv, *, batch, seq, tile, tiles):
    rows = batch * seq
    nq = seq // tile
    steps = nq // tiles
    has_meta = meta_kv is not None
    assert (N_HEADS // HEAD_GROUP) % 2 == 0 and HEAD_GROUP % 2 == 0 and nq % tiles == 0
    diag = min(tile, DIAG_BLOCK)
    idx = np.arange(diag)
    tri = np.where(idx[:, None] <= idx[None, :], 0.0, MASK_VALUE).astype(np.float32)
    in_specs = [
        pl.BlockSpec((N_HEADS * HEAD_PAD, tiles * tile), lambda b, i: (0, b * steps + i)),
        pl.BlockSpec((N_HEADS, seq, HEAD_PAD), lambda b, i: (0, b, 0)),
        pl.BlockSpec((nq, ATT_WIDTH, tile), lambda b, i: (b, 0, 0)),
        _const_spec((diag, diag)),
    ]
    args = [q, k, vt, tri]
    if has_meta:
        km, vtm = meta_kv
        in_specs += [_const_spec(km.shape), _const_spec(vtm.shape)]
        args += [km, vtm]
    body = functools.partial(_attention_body, tile=tile, tiles=tiles, diag=diag, has_meta=has_meta,
                             group=HEAD_GROUP)
    return pl.pallas_call(
        body,
        grid=(batch, steps),
        in_specs=in_specs,
        out_specs=pl.BlockSpec((tiles * tile, ATT_WIDTH), lambda b, i: (b * steps + i, 0)),
        out_shape=jax.ShapeDtypeStruct((rows, ATT_WIDTH), jnp.bfloat16),
        scratch_shapes=[
            pltpu.VMEM((2, HEAD_GROUP, tile + N_META, tile), jnp.float32),
            pltpu.VMEM((2, HEAD_GROUP, 1, tile), jnp.float32),
            pltpu.VMEM((tiles * N_HEADS, 1, tile), jnp.float32),
            pltpu.VMEM((tiles * N_HEADS, V_DIM + ONES_ROWS, tile), jnp.float32),
        ],
        compiler_params=pltpu.CompilerParams(
            dimension_semantics=("parallel", "parallel"), vmem_limit_bytes=VMEM_LIMIT),
        name="attention",
    )(*args)


def _post_attention_body(h_ref, a_ref, b_ref, gmix_ref, wgates_ref, wpa_ref, wpb_ref, wo_ref,
                         gffn_ref, wgate_ref, wup_ref, wdown_ref, gfin_ref, o_ref, *, final_norm):
    tile = h_ref.shape[0]
    step = min(tile, max(tile // POST_SPLIT, MXU_ROWS))
    parts = [pl.ds(r * step, step) for r in range(tile // step)]
    pa = [_dot(a_ref[r, :], wpa_ref[...]) for r in parts]
    pb = [_dot(b_ref[r, :], wpb_ref[...]) for r in parts]
    hs = [h_ref[r, :] for r in parts]
    hn = [_rmsnorm(h, gmix_ref[...]).astype(jnp.bfloat16) for h in hs]
    gates = [jax.nn.sigmoid(_dot(x, wgates_ref[...])) for x in hn]
    merged = [(g[:, :D_MODEL] * xa + g[:, D_MODEL:] * xb).astype(jnp.bfloat16)
              for g, xa, xb in zip(gates, pa, pb)]
    hs = [h + _dot(m, wo_ref[...]) for h, m in zip(hs, merged)]
    hn = [_rmsnorm(h, gffn_ref[...]).astype(jnp.bfloat16) for h in hs]
    act = [(jax.nn.silu(_dot(x, wgate_ref[...])) * _dot(x, wup_ref[...])).astype(jnp.bfloat16)
           for x in hn]
    hs = [h + _dot(x, wdown_ref[...]) for h, x in zip(hs, act)]
    for r, h in zip(parts, hs):
        o_ref[r, :] = _rmsnorm(h, gfin_ref[...]) if final_norm else h


def _post_attention(h, a, b, lw, layer, g_final, *, tile, final_norm):
    rows = h.shape[0]
    row_map = lambda r: (r, 0)
    body = functools.partial(_post_attention_body, final_norm=final_norm)
    return pl.pallas_call(
        body,
        grid=(rows // tile,),
        in_specs=[
            pl.BlockSpec((tile, D_MODEL), row_map),
            pl.BlockSpec((tile, POOL_WIDTH), row_map),
            pl.BlockSpec((tile, ATT_WIDTH), row_map),
            _const_spec((1, D_MODEL), layer),
            _const_spec((D_MODEL, 2 * D_MODEL), layer),
            _const_spec((POOL_WIDTH, D_MODEL), layer),
            _const_spec((ATT_WIDTH, D_MODEL), layer),
            _const_spec((D_MODEL, D_MODEL), layer),
            _const_spec((1, D_MODEL), layer),
            _const_spec((D_MODEL, D_FF), layer),
            _const_spec((D_MODEL, D_FF), layer),
            _const_spec((D_FF, D_MODEL), layer),
            _const_spec((1, D_MODEL)),
        ],
        out_specs=pl.BlockSpec((tile, D_MODEL), row_map),
        out_shape=jax.ShapeDtypeStruct((rows, D_MODEL), jnp.float32),
        compiler_params=pltpu.CompilerParams(
            dimension_semantics=("parallel",), vmem_limit_bytes=VMEM_LIMIT),
        name="post_attention",
    )(h, a, b, lw["g_mix"], lw["w_in_gates"], lw["w_pa"], lw["w_pb"], lw["w_o"], lw["g_ffn"],
      lw["w_gate"], lw["w_up"], lw["w_down"], g_final)


def _head_lanes():
    return [(QK_NOPE, QK_NOPE + ROPE_HALF), (QK_NOPE + ROPE_HALF, QK_DIM), (0, QK_NOPE),
            (None, HEAD_PAD - QK_DIM)]


def _place_head_columns(w, keep, xp=jnp):
    parts = []
    for start, stop in _head_lanes():
        if start is None or not keep(start):
            width = stop if start is None else stop - start
            parts.append(xp.zeros(w.shape[:-1] + (width,), w.dtype))
        else:
            parts.append(w[..., start:stop])
    return xp.concatenate(parts, axis=-1)


def _prepare_weights(norm_mix_g, w_in, pool_w, pool_scale, q_norm_g, kv_norm_g, w_uq, w_ukv,
                     w_pa, w_pb, w_o, norm_ffn_g, w_gate, w_up, w_down):
    bf = jnp.bfloat16
    depth = w_in.shape[0]
    o_q = POOL_WIDTH + Q_RANK
    o_kv = o_q + KV_RANK
    o_kr = o_kv + QK_ROPE
    is_nope = lambda col: col < QK_NOPE
    kr_cols = _place_head_columns(jnp.pad(w_in[..., o_kv:o_kr], ((0, 0), (0, 0), (QK_NOPE, 0))),
                                  lambda col: not is_nope(col))
    w_in_a = jnp.concatenate([w_in[..., :o_kv], kr_cols], axis=-1).astype(bf)
    w_q = _place_head_columns(w_uq.reshape(depth, Q_RANK, N_HEADS, QK_DIM), lambda col: True)
    ukv = w_ukv.reshape(depth, KV_RANK, N_HEADS, QK_NOPE + V_DIM)
    w_k = _place_head_columns(jnp.pad(ukv[..., :QK_NOPE], [(0, 0)] * 3 + [(0, QK_ROPE)]), is_nope)
    w_vt = jnp.swapaxes(ukv[..., QK_NOPE:].reshape(depth, KV_RANK, ATT_WIDTH), 1, 2)
    row = lambda g: g[:, None, :]
    return dict(
        g_mix=row(norm_mix_g), w_in_a=w_in_a, w_in_gates=w_in[..., o_kr:].astype(bf),
        pool_w=pool_w.astype(bf), pool_scale=row(pool_scale),
        q_g=row(q_norm_g), kv_g=row(kv_norm_g),
        w_qt=jnp.swapaxes(w_q.reshape(depth, Q_RANK, N_HEADS * HEAD_PAD), 1, 2).astype(bf),
        w_k=w_k.reshape(depth, KV_RANK, N_HEADS * HEAD_PAD).astype(bf), w_vt=w_vt.astype(bf),
        w_pa=w_pa.astype(bf), w_pb=w_pb.astype(bf), w_o=w_o.astype(bf),
        g_ffn=row(norm_ffn_g), w_gate=w_gate.astype(bf), w_up=w_up.astype(bf),
        w_down=w_down.astype(bf))


def _rope_tables(length):
    f32 = np.float32
    inv = f32(1.0) / (f32(ROPE_THETA) ** (np.arange(0, QK_ROPE, 2, dtype=f32) / f32(QK_ROPE)))
    ang = np.arange(length, dtype=f32)[:, None] * inv[None, :]
    cos, sin = np.cos(ang).astype(f32), np.sin(ang).astype(f32)
    ones = np.ones((length, QK_NOPE), f32)
    cos_t = np.asarray(_place_head_columns(np.concatenate([ones, cos, cos], axis=1),
                                           lambda col: True, xp=np))
    sin_t = np.asarray(_place_head_columns(np.concatenate([0 * ones, -sin, sin], axis=1),
                                           lambda col: True, xp=np))
    return cos_t * f32(Q_SCALE), sin_t * f32(Q_SCALE), cos_t, sin_t


def _tile_q_tables(tables, tile):
    cos_q, sin_q, cos_k, sin_k = tables
    per_tile = lambda t: np.ascontiguousarray(t.reshape(-1, tile, HEAD_PAD).transpose(0, 2, 1))
    return per_tile(cos_q), per_tile(sin_q), cos_k, sin_k


def kernel(x, meta_tokens, norm_mix_g, w_in, pool_w, pool_scale, q_norm_g, kv_norm_g, w_uq, w_ukv,
           w_pa, w_pb, w_o, norm_ffn_g, w_gate, w_up, w_down, final_norm_g):
    batch, seq, _ = x.shape
    depth = w_in.shape[0]
    tile = ROW_TILE
    tables = _rope_tables(N_META + max(seq, META_TILE))
    meta_tables = _tile_q_tables(tuple(t[:META_TILE] for t in tables), META_TILE)
    main_tables = _tile_q_tables(tuple(t[N_META:N_META + seq] for t in tables), PRE_TILE)
    g_final = final_norm_g[None]
    lw = _prepare_weights(norm_mix_g, w_in, pool_w, pool_scale, q_norm_g, kv_norm_g, w_uq, w_ukv,
                          w_pa, w_pb, w_o, norm_ffn_g, w_gate, w_up, w_down)

    h = x.reshape(batch * seq, D_MODEL)
    hm = jnp.pad(meta_tokens.astype(x.dtype), ((0, META_TILE - N_META), (0, 0)))
    zero_halo = np.zeros((MAX_WINDOW, POOL_WIDTH), np.float32)
    for i in range(depth):
        last = i == depth - 1
        qm, km, vtm, am, um = _pre_attention(hm, lw, i, meta_tables, zero_halo, batch=1,
                                             seq=META_TILE, tile=META_TILE, kv_tile=META_TILE,
                                             pos_offset=0,
                                             emit_u=True)
        if not last:
            bm = _attention(qm, km, vtm, None, batch=1, seq=META_TILE, tile=META_TILE, tiles=1)
            hm = _post_attention(hm, am, bm, lw, i, g_final, tile=META_TILE, final_norm=False)
        q, k, vt, a = _pre_attention(h, lw, i, main_tables, um, batch=batch, seq=seq,
                                     tile=PRE_TILE, kv_tile=ATTN_TILE, pos_offset=N_META,
                                     emit_u=False)
        b = _attention(q, k, vt, (km, vtm), batch=batch, seq=seq, tile=ATTN_TILE, tiles=ATTN_TILES)
        h = _post_attention(h, a, b, lw, i, g_final, tile=tile, final_norm=last)
    return h.reshape(batch, seq, D_MODEL)
```
